```python
import jax, jax.numpy as jnp
from jax import lax
import numpy as np

D_MODEL = 1024
BATCH = 2
SEQ = 8192
DEPTH = 1

A_WIDTH = D_MODEL
A_GROUPS = 8
A_GROUP_DIM = A_WIDTH // A_GROUPS
A_CHUNK = 128
B_HEADS = 4
B_KEY_DIM = D_MODEL // 2
B_VAL_DIM = D_MODEL
B_HEAD_K = B_KEY_DIM // B_HEADS
B_HEAD_V = B_VAL_DIM // B_HEADS
B_GATE_RANK = 16
B_GATE_NORMALIZER = 16.0
B_CHUNK = 64
EPS = 1e-6
IN_SPLITS = (A_WIDTH, A_WIDTH, A_WIDTH, B_KEY_DIM, B_KEY_DIM, B_VAL_DIM, B_VAL_DIM, B_GATE_RANK, D_MODEL, D_MODEL)
IN_WIDTH = 3 * A_WIDTH + 2 * B_KEY_DIM + 2 * B_VAL_DIM + B_GATE_RANK + 2 * D_MODEL

kernel_name = "hybrid_gmlp_gla_gated_merge"


def rms_norm(x, g):
    xf = x.astype(jnp.float32)
    y = xf * lax.rsqrt(jnp.mean(xf * xf, axis=-1, keepdims=True) + EPS)
    return (y * g.astype(jnp.float32)).astype(x.dtype)


def layer_norm(x, g, b):
    xf = x.astype(jnp.float32)
    mu = jnp.mean(xf, axis=-1, keepdims=True)
    xc = xf - mu
    y = xc * lax.rsqrt(jnp.mean(xc * xc, axis=-1, keepdims=True) + EPS)
    return (y * g.astype(jnp.float32) + b.astype(jnp.float32)).astype(x.dtype)


def split_offsets():
    offs, acc = [], 0
    for s in IN_SPLITS[:-1]:
        acc += s
        offs.append(acc)
    return offs


def spatial_gating(u, v, w_s, b_s, ln_g, ln_b):
    bsz, seq, _ = v.shape
    n = seq // A_CHUNK
    v = layer_norm(v, ln_g, ln_b).reshape(bsz, n, A_CHUNK, A_GROUPS, A_GROUP_DIM)
    mask = jnp.tril(jnp.ones((A_CHUNK, A_CHUNK), dtype=bool))
    w = jnp.where(mask, w_s, jnp.zeros_like(w_s)).astype(v.dtype)
    mixed = jnp.einsum('gts,bnsgd->bntgd', w, v) + b_s.T.astype(v.dtype)[None, None, :, :, None]
    return u * mixed.reshape(bsz, seq, A_WIDTH)


def gla_chunked(q, k, v, log_a):
    bsz, seq = q.shape[:2]
    n = seq // B_CHUNK
    out_dtype = v.dtype

    def to_chunks(t):
        return t.reshape(bsz, n, B_CHUNK, B_HEADS, t.shape[-1]).transpose(1, 0, 3, 2, 4).astype(jnp.float32)

    qc = to_chunks(q * (B_HEAD_K ** -0.5))
    kc, vc, gc = to_chunks(k), to_chunks(v), to_chunks(log_a)
    mask = jnp.tril(jnp.ones((B_CHUNK, B_CHUNK), dtype=bool))[:, :, None]

    def step(state, inp):
        qi, ki, vi, gi = inp
        b = jnp.cumsum(gi, axis=2)
        o_inter = jnp.einsum('bhtk,bhkv->bhtv', qi * jnp.exp(b), state)
        diff = b[:, :, :, None, :] - b[:, :, None, :, :]
        decay = jnp.exp(jnp.where(mask, diff, -jnp.inf))
        scores = jnp.einsum('bhtk,bhsk,bhtsk->bhts', qi, ki, decay)
        o_intra = jnp.einsum('bhts,bhsv->bhtv', scores, vi)
        b_last = b[:, :, -1:, :]
        k_dec = ki * jnp.exp(b_last - b)
        new_state = jnp.exp(b_last[:, :, 0, :])[..., None] * state + jnp.einsum('bhsk,bhsv->bhkv', k_dec, vi)
        return new_state, o_inter + o_intra

    state0 = jnp.zeros((bsz, B_HEADS, B_HEAD_K, B_HEAD_V), jnp.float32)
    _, out = lax.scan(step, state0, (qc, kc, vc, gc))
    return out.transpose(1, 0, 3, 2, 4).reshape(bsz, seq, B_HEADS, B_HEAD_V).astype(out_dtype)


def setup_inputs(seed: int = 0) -> dict:
    key = jax.random.key(seed)
    ks = jax.random.split(key, 24)
    nrm = lambda k, shape, s: jax.random.normal(k, shape, jnp.float32) * s
    L, D = DEPTH, D_MODEL
    return {
        "x": nrm(ks[0], (BATCH, SEQ, D), 1.0),
        "c": nrm(ks[1], (BATCH, D), 1.0),
        "w_ada": nrm(ks[2], (L, D, 3 * D), 0.5 * D ** -0.5),
        "b_ada": nrm(ks[3], (L, 3 * D), 0.01),
        "norm_g": 1.0 + nrm(ks[4], (L, D), 0.02),
        "w_in": nrm(ks[5], (L, D, IN_WIDTH), D ** -0.5),
        "a_ln_g": 1.0 + nrm(ks[6], (L, A_WIDTH), 0.02),
        "a_ln_b": nrm(ks[7], (L, A_WIDTH), 0.01),
        "a_w_s": nrm(ks[8], (L, A_GROUPS, A_CHUNK, A_CHUNK), A_CHUNK ** -0.5),
        "a_b_s": 1.0 + nrm(ks[9], (L, A_GROUPS, A_CHUNK), 0.02),
        "b_w_gate2": nrm(ks[10], (L, B_GATE_RANK, B_KEY_DIM), B_GATE_RANK ** -0.5),
        "b_b_gate": nrm(ks[11], (L, B_KEY_DIM), 0.01),
        "b_norm_g": 1.0 + nrm(ks[12], (L, B_HEAD_V), 0.02),
        "w_proj_a": nrm(ks[13], (L, A_WIDTH, D), A_WIDTH ** -0.5),
        "w_proj_b": nrm(ks[14], (L, B_VAL_DIM, D), B_VAL_DIM ** -0.5),
        "b_merge": nrm(ks[15], (L, 2 * D), 0.01),
        "w_out": nrm(ks[16], (L, D, D), D ** -0.5),
        "final_g": 1.0 + nrm(ks[17], (D,), 0.02),
    }


def reference(x, c, w_ada, b_ada, norm_g, w_in, a_ln_g, a_ln_b, a_w_s, a_b_s, b_w_gate2, b_b_gate, b_norm_g, w_proj_a, w_proj_b, b_merge, w_out, final_g):
    bsz, seq, _ = x.shape
    h = x
    offs = split_offsets()
    for l in range(DEPTH):
        mod = jnp.einsum('bd,de->be', jax.nn.silu(c), w_ada[l]) + b_ada[l]
        shift, scale, gate = jnp.split(mod[:, None, :], 3, axis=-1)
        hn = rms_norm(h, norm_g[l]) * (1.0 + scale) + shift

        proj = jnp.einsum('bsd,de->bse', hn, w_in[l])
        a_u, a_v, a_z, b_q, b_k, b_v, b_z, b_g1, merge_logits_a, merge_logits_b = jnp.split(proj, offs, axis=-1)

        a_u = jax.nn.gelu(a_u, approximate=False)
        a_v = jax.nn.gelu(a_v, approximate=False)
        y_a = spatial_gating(a_u, a_v, a_w_s[l], a_b_s[l], a_ln_g[l], a_ln_b[l]) * jax.nn.silu(a_z)
        y_a = jnp.einsum('bse,ed->bsd', y_a, w_proj_a[l])

        gk = jnp.einsum('bsr,rk->bsk', b_g1, b_w_gate2[l]) + b_b_gate[l]
        log_a = jax.nn.log_sigmoid(gk.astype(jnp.float32)) / B_GATE_NORMALIZER
        o_b = gla_chunked(b_q.reshape(bsz, seq, B_HEADS, B_HEAD_K),
                          b_k.reshape(bsz, seq, B_HEADS, B_HEAD_K),
                          b_v.reshape(bsz, seq, B_HEADS, B_HEAD_V),
                          log_a.reshape(bsz, seq, B_HEADS, B_HEAD_K))
        o_b = rms_norm(o_b, b_norm_g[l]) * jax.nn.silu(b_z.reshape(bsz, seq, B_HEADS, B_HEAD_V))
        y_b = jnp.einsum('bse,ed->bsd', o_b.reshape(bsz, seq, B_VAL_DIM), w_proj_b[l])

        g_a = jax.nn.sigmoid(merge_logits_a + b_merge[l, :D_MODEL])
        g_b = jax.nn.sigmoid(merge_logits_b + b_merge[l, D_MODEL:])
        merged = g_a * y_a + g_b * y_b
        out = jnp.einsum('bsd,de->bse', merged, w_out[l])
        h = h + gate * out
    return rms_norm(h, final_g)
```

```python
import functools
import math

import jax
import jax.numpy as jnp
from jax import lax
from jax.experimental import pallas as pl
from jax.experimental.pallas import tpu as pltpu

D_MODEL = 1024
A_GROUPS = 8
A_CHUNK = 128
A_GROUP_DIM = D_MODEL // A_GROUPS
B_HEADS = 4
B_KEY_DIM = D_MODEL // 2
B_VAL_DIM = D_MODEL
B_HEAD_K = B_KEY_DIM // B_HEADS
B_HEAD_V = B_VAL_DIM // B_HEADS
B_GATE_RANK = 16
B_GATE_NORMALIZER = 16.0
EPS = 1e-6

LANES = 128
SUBLANES = 8
TILE_M = 256
VMEM_LIMIT_BYTES = 56 * 1024 * 1024

OFF_U, OFF_V, OFF_Z = 0, 1024, 2048
OFF_Q, OFF_K = 3072, 3584
OFF_BV, OFF_BZ = 4096, 5120
OFF_MA, OFF_MB = 6144, 7168
OFF_G1 = 8192
W_IN_PACKED = OFF_G1 + LANES

F32 = jnp.float32
BF16 = jnp.bfloat16
NEG_BIG = -1e30


def _dot(a, b):
    return jnp.dot(a, b, preferred_element_type=F32)


def _dot_nt(a, b):
    return lax.dot_general(a, b, (((1,), (1,)), ((), ())), preferred_element_type=F32)


def _dot_tn(a, b):
    return lax.dot_general(a, b, (((0,), (0,)), ((), ())), preferred_element_type=F32)


def _split_bf16(a):
    hi = a.astype(BF16)
    lo = (a - hi.astype(F32)).astype(BF16)
    return hi, lo


def _gelu(a):
    return 0.5 * a * (1.0 + lax.erf(a * (1.0 / math.sqrt(2.0))))


def _silu(a):
    return a * jax.nn.sigmoid(a)


def _bcast_row_in_blocks(a, blk, row):
    m, n = a.shape
    a3 = a.reshape(m // blk, blk, n)
    return jnp.broadcast_to(a3[:, row:row + 1, :], (m // blk, blk, n)).reshape(m, n)


def _mod_kernel(c_ref, w_ref, b_ref, o_ref):
    c = c_ref[...]
    o_ref[...] = jnp.dot(_silu(c), w_ref[...], preferred_element_type=F32,
                         precision=lax.Precision.HIGHEST) + b_ref[...]


def _gla_head(qs, k, v_b, b, state_ref, h, row, col):
    c = qs.shape[0]
    state = state_ref[h]
    o = _dot((qs * jnp.exp(b)).astype(BF16), state.astype(BF16))

    scores = jnp.zeros((c, c), F32)
    m = SUBLANES
    while m < c:
        ref_b = _bcast_row_in_blocks(b, 2 * m, m - 1)
        e = jnp.exp(-jnp.abs(b - ref_b))
        siblings = ((row // m) == (col // m) + 1) & ((col // m) % 2 == 0)
        scores = jnp.where(siblings, _dot_nt((qs * e).astype(BF16), (k * e).astype(BF16)), scores)
        m *= 2

    row_in = row % SUBLANES
    lane = lax.broadcasted_iota(jnp.int32, (c, LANES), 1)
    diag = jnp.zeros((c, LANES), F32)
    for j in range(SUBLANES):
        b_j = _bcast_row_in_blocks(b, SUBLANES, j)
        k_j = _bcast_row_in_blocks(k, SUBLANES, j)
        e = jnp.exp(jnp.where(row_in >= j, b - b_j, NEG_BIG))
        s_j = jnp.sum(qs * k_j * e, axis=-1, keepdims=True)
        diag = jnp.where(lane == j, s_j, diag)
    sel = (lax.broadcasted_iota(jnp.int32, (LANES, c), 0)
           == lax.broadcasted_iota(jnp.int32, (LANES, c), 1) % SUBLANES)
    diag_full = _dot(diag.astype(BF16), sel.astype(BF16))
    scores_b = jnp.where((row // SUBLANES) == (col // SUBLANES), diag_full.astype(BF16),
                         scores.astype(BF16))
    o = o + _dot(scores_b, v_b)

    b_last = b[c - 1:c, :]
    k_dec = (k * jnp.exp(b_last - b)).astype(BF16)
    decay_col = jnp.transpose(jnp.broadcast_to(jnp.exp(b_last), (LANES, B_HEAD_K)))[:, 0:1]
    state_ref[h] = decay_col * state + _dot_tn(k_dec, v_b)
    return o


def _block_kernel(x_ref, mod_ref, ng_ref, w_in_ref, alng_ref, alnb_ref, aws_ref, abs_ref,
                  wg2_ref, bbg_ref, bng_ref, wpa_ref, wpb_ref, bm_ref, wout_ref, fg_ref,
                  o_ref, state_ref):
    @pl.when(pl.program_id(1) == 0)
    def _():
        state_ref[...] = jnp.zeros_like(state_ref)

    tm = x_ref.shape[1]
    x = x_ref[0]
    mod = mod_ref[0]
    shift = mod[:, 0:D_MODEL]
    scale = mod[:, D_MODEL:2 * D_MODEL]
    gate = mod[:, 2 * D_MODEL:3 * D_MODEL]

    y = x * lax.rsqrt(jnp.mean(x * x, axis=-1, keepdims=True) + EPS) * ng_ref[...]
    hn_b = (y * (1.0 + scale) + shift).astype(BF16)

    def proj(lo, width):
        return _dot(hn_b, w_in_ref[:, lo:lo + width])

    a_u = _gelu(proj(OFF_U, D_MODEL))
    a_v = _gelu(proj(OFF_V, D_MODEL))
    mu = jnp.mean(a_v, axis=-1, keepdims=True)
    a_vc = a_v - mu
    a_vn = a_vc * lax.rsqrt(jnp.mean(a_vc * a_vc, axis=-1, keepdims=True) + EPS)
    a_vn = (a_vn * alng_ref[...] + alnb_ref[...]).astype(BF16)
    tri = (lax.broadcasted_iota(jnp.int32, (A_CHUNK, A_CHUNK), 0)
           >= lax.broadcasted_iota(jnp.int32, (A_CHUNK, A_CHUNK), 1))
    w_s = [jnp.where(tri, aws_ref[g], 0.0).astype(BF16) for g in range(A_GROUPS)]
    rows = []
    for ci in range(tm // A_CHUNK):
        r0 = ci * A_CHUNK
        rows.append(jnp.concatenate(
            [_dot(w_s[g], a_vn[r0:r0 + A_CHUNK, g * A_GROUP_DIM:(g + 1) * A_GROUP_DIM]) + abs_ref[g]
             for g in range(A_GROUPS)], axis=1))
    mixed = jnp.concatenate(rows, axis=0)
    y_a = a_u * mixed * _silu(proj(OFF_Z, D_MODEL))
    y_a = _dot(y_a.astype(BF16), wpa_ref[...])
    g_a = jax.nn.sigmoid(proj(OFF_MA, D_MODEL) + bm_ref[:, 0:D_MODEL])
    merged = g_a * y_a

    q_all = proj(OFF_Q, B_KEY_DIM) * (B_HEAD_K ** -0.5)
    k_all = proj(OFF_K, B_KEY_DIM)
    v_all = proj(OFF_BV, B_VAL_DIM).astype(BF16)
    g1 = proj(OFF_G1, LANES)
    gk = _dot(g1.astype(BF16), wg2_ref[...]) + bbg_ref[...]
    log_a = (jnp.minimum(gk, 0.0) - jnp.log1p(jnp.exp(-jnp.abs(gk)))) * (1.0 / B_GATE_NORMALIZER)
    row = lax.broadcasted_iota(jnp.int32, (tm, 1), 0)
    col = lax.broadcasted_iota(jnp.int32, (1, tm), 1)
    tril_b = (lax.broadcasted_iota(jnp.int32, (tm, tm), 0)
              >= lax.broadcasted_iota(jnp.int32, (tm, tm), 1)).astype(BF16)
    la_hi, la_lo = _split_bf16(log_a)
    b_all = _dot(tril_b, la_hi) + _dot(tril_b, la_lo)

    heads = []
    for h in range(B_HEADS):
        ks = slice(h * B_HEAD_K, (h + 1) * B_HEAD_K)
        vs = slice(h * B_HEAD_V, (h + 1) * B_HEAD_V)
        o_h = _gla_head(q_all[:, ks], k_all[:, ks], v_all[:, vs], b_all[:, ks], state_ref, h, row, col)
        o_h = o_h * lax.rsqrt(jnp.mean(o_h * o_h, axis=-1, keepdims=True) + EPS) * bng_ref[...]
        heads.append(o_h)
    o_b = jnp.concatenate(heads, axis=1) * _silu(proj(OFF_BZ, B_VAL_DIM))
    y_b = _dot(o_b.astype(BF16), wpb_ref[...])
    g_b = jax.nn.sigmoid(proj(OFF_MB, D_MODEL) + bm_ref[:, D_MODEL:2 * D_MODEL])
    merged = merged + g_b * y_b

    out = _dot(merged.astype(BF16), wout_ref[...])
    hres = x + gate * out
    o_ref[0] = hres * lax.rsqrt(jnp.mean(hres * hres, axis=-1, keepdims=True) + EPS) * fg_ref[...]


def _const_spec(shape):
    nd = len(shape)
    return pl.BlockSpec(shape, lambda b, j: (0,) * nd, pipeline_mode=pl.Buffered(1))


def kernel(x, c, w_ada, b_ada, norm_g, w_in, a_ln_g, a_ln_b, a_w_s, a_b_s, b_w_gate2, b_b_gate, b_norm_g, w_proj_a, w_proj_b, b_merge, w_out, final_g):
    bsz, seq, d = x.shape
    assert d == D_MODEL and seq % TILE_M == 0 and w_ada.shape[0] == 1
    l = 0

    c_pad = jnp.zeros((SUBLANES, d), F32).at[:bsz].set(c)
    mod_cols = 1024
    mod = pl.pallas_call(
        _mod_kernel,
        grid=(3 * d // mod_cols,),
        in_specs=[pl.BlockSpec((SUBLANES, d), lambda n: (0, 0)),
                  pl.BlockSpec((d, mod_cols), lambda n: (0, n)),
                  pl.BlockSpec((1, mod_cols), lambda n: (0, n))],
        out_specs=pl.BlockSpec((SUBLANES, mod_cols), lambda n: (0, n)),
        out_shape=jax.ShapeDtypeStruct((SUBLANES, 3 * d), F32),
        name="adaln_mod",
    )(c_pad, w_ada[l], b_ada[l][None, :])
    mod = mod[:bsz, None, :]

    wl = w_in[l]
    g1_lo = 6 * D_MODEL
    w_packed = jnp.concatenate(
        [wl[:, :g1_lo], wl[:, g1_lo + B_GATE_RANK:], wl[:, g1_lo:g1_lo + B_GATE_RANK],
         jnp.zeros((d, LANES - B_GATE_RANK), F32)], axis=1).astype(BF16)
    wg2 = jnp.zeros((LANES, B_KEY_DIM), F32).at[:B_GATE_RANK].set(b_w_gate2[l]).astype(BF16)
    abs_full = jnp.broadcast_to(a_b_s[l][:, :, None], (A_GROUPS, A_CHUNK, A_GROUP_DIM))

    row2 = lambda v: v.reshape(1, -1)
    operands = (
        x, mod, row2(norm_g[l]), w_packed, row2(a_ln_g[l]), row2(a_ln_b[l]), a_w_s[l], abs_full,
        wg2, row2(b_b_gate[l]), row2(b_norm_g[l]), w_proj_a[l].astype(BF16), w_proj_b[l].astype(BF16),
        row2(b_merge[l]), w_out[l].astype(BF16), row2(final_g),
    )
    in_specs = [
        pl.BlockSpec((1, TILE_M, d), lambda b, j: (b, j, 0)),
        pl.BlockSpec((1, 1, 3 * d), lambda b, j: (b, 0, 0)),
    ] + [_const_spec(op.shape) for op in operands[2:]]

    return pl.pallas_call(
        _block_kernel,
        grid=(bsz, seq // TILE_M),
        in_specs=in_specs,
        out_specs=pl.BlockSpec((1, TILE_M, d), lambda b, j: (b, j, 0)),
        out_shape=jax.ShapeDtypeStruct((bsz, seq, d), x.dtype),
        scratch_shapes=[pltpu.VMEM((B_HEADS, B_HEAD_K, B_HEAD_V), F32)],
        compiler_params=pltpu.CompilerParams(
            dimension_semantics=("arbitrary", "arbitrary"),
            vmem_limit_bytes=VMEM_LIMIT_BYTES),
        name="hybrid_block",
    )(*operands)
```

```python
import functools
import math

import jax
import jax.numpy as jnp
from jax import lax
from jax.experimental import pallas as pl
from jax.experimental.pallas import tpu as pltpu

D_MODEL = 1024
A_GROUPS = 8
A_CHUNK = 128
A_GROUP_DIM = D_MODEL // A_GROUPS
B_HEADS = 4
B_KEY_DIM = D_MODEL // 2
B_VAL_DIM = D_MODEL
B_HEAD_K = B_KEY_DIM // B_HEADS
B_HEAD_V = B_VAL_DIM // B_HEADS
B_GATE_RANK = 16
B_GATE_NORMALIZER = 16.0
EPS = 1e-6

LANES = 128
SUBLANES = 8
TILE_M = 256
VMEM_LIMIT_BYTES = 56 * 1024 * 1024

OFF_U, OFF_V, OFF_Z = 0, 1024, 2048
OFF_Q, OFF_K = 3072, 3584
OFF_BV, OFF_BZ = 4096, 5120
W_MAIN = 6 * D_MODEL
OFF_MA, OFF_MB = 0, 1024

F32 = jnp.float32
BF16 = jnp.bfloat16
NEG_BIG = -1e30


def _dot(a, b):
    return jnp.dot(a, b, preferred_element_type=F32)


def _dot_nt(a, b):
    return lax.dot_general(a, b, (((1,), (1,)), ((), ())), preferred_element_type=F32)


def _dot_tn(a, b):
    return lax.dot_general(a, b, (((0,), (0,)), ((), ())), preferred_element_type=F32)


def _split_bf16(a):
    hi = a.astype(BF16)
    lo = (a - hi.astype(F32)).astype(BF16)
    return hi, lo


def _gelu(a):
    return 0.5 * a * (1.0 + lax.erf(a * (1.0 / math.sqrt(2.0))))


def _silu(a):
    return a * jax.nn.sigmoid(a)


def _bcast_row_in_blocks(a, blk, row):
    m, n = a.shape
    a3 = a.reshape(m // blk, blk, n)
    return jnp.broadcast_to(a3[:, row:row + 1, :], (m // blk, blk, n)).reshape(m, n)


def _mod_kernel(c_ref, w_ref, b_ref, o_ref):
    c = c_ref[...]
    o_ref[...] = jnp.dot(_silu(c), w_ref[...], preferred_element_type=F32,
                         precision=lax.Precision.HIGHEST) + b_ref[...]


def _gla_head(qs, k, v_b, b, state_ref, h, row, col):
    c = qs.shape[0]
    state = state_ref[h]
    o = _dot((qs * jnp.exp(b)).astype(BF16), state.astype(BF16))

    scores = jnp.zeros((c, c), F32)
    m = SUBLANES
    while m < c:
        ref_b = _bcast_row_in_blocks(b, 2 * m, m - 1)
        e = jnp.exp(-jnp.abs(b - ref_b))
        siblings = ((row // m) == (col // m) + 1) & ((col // m) % 2 == 0)
        scores = jnp.where(siblings, _dot_nt((qs * e).astype(BF16), (k * e).astype(BF16)), scores)
        m *= 2

    row_in = row % SUBLANES
    lane = lax.broadcasted_iota(jnp.int32, (c, LANES), 1)
    diag = jnp.zeros((c, LANES), F32)
    for j in range(SUBLANES):
        b_j = _bcast_row_in_blocks(b, SUBLANES, j)
        k_j = _bcast_row_in_blocks(k, SUBLANES, j)
        e = jnp.exp(jnp.where(row_in >= j, b - b_j, NEG_BIG))
        s_j = jnp.sum(qs * k_j * e, axis=-1, keepdims=True)
        diag = jnp.where(lane == j, s_j, diag)
    sel = (lax.broadcasted_iota(jnp.int32, (LANES, c), 0)
           == lax.broadcasted_iota(jnp.int32, (LANES, c), 1) % SUBLANES)
    diag_full = _dot(diag.astype(BF16), sel.astype(BF16))
    scores_b = jnp.where((row // SUBLANES) == (col // SUBLANES), diag_full.astype(BF16),
                         scores.astype(BF16))
    o = o + _dot(scores_b, v_b)

    b_last = b[c - 1:c, :]
    k_dec = (k * jnp.exp(b_last - b)).astype(BF16)
    decay_col = jnp.transpose(jnp.broadcast_to(jnp.exp(b_last), (LANES, B_HEAD_K)))[:, 0:1]
    state_ref[h] = decay_col * state + _dot_tn(k_dec, v_b)
    return o


def _block_kernel(x_ref, mod_ref, ng_ref, w_in_ref, w_mg_ref, w_g1_ref, alng_ref, alnb_ref, aws_ref, abs_ref,
                  wg2_ref, bbg_ref, bng_ref, wpa_ref, wpb_ref, bm_ref, wout_ref, fg_ref,
                  o_ref, state_ref):
    @pl.when(pl.program_id(1) == 0)
    def _():
        state_ref[...] = jnp.zeros_like(state_ref)

    tm = x_ref.shape[1]
    x = x_ref[0]
    mod = mod_ref[0]
    shift = mod[:, 0:D_MODEL]
    scale = mod[:, D_MODEL:2 * D_MODEL]
    gate = mod[:, 2 * D_MODEL:3 * D_MODEL]

    y = x * lax.rsqrt(jnp.mean(x * x, axis=-1, keepdims=True) + EPS) * ng_ref[...]
    hn_b = (y * (1.0 + scale) + shift).astype(BF16)

    def proj(lo, width, w_ref=w_in_ref):
        return _dot(hn_b, w_ref[:, lo:lo + width])

    a_u = _gelu(proj(OFF_U, D_MODEL))
    a_v = _gelu(proj(OFF_V, D_MODEL))
    mu = jnp.mean(a_v, axis=-1, keepdims=True)
    a_vc = a_v - mu
    a_vn = a_vc * lax.rsqrt(jnp.mean(a_vc * a_vc, axis=-1, keepdims=True) + EPS)
    a_vn = (a_vn * alng_ref[...] + alnb_ref[...]).astype(BF16)
    tri = (lax.broadcasted_iota(jnp.int32, (A_CHUNK, A_CHUNK), 0)
           >= lax.broadcasted_iota(jnp.int32, (A_CHUNK, A_CHUNK), 1))
    w_s = [jnp.where(tri, aws_ref[g], 0.0).astype(BF16) for g in range(A_GROUPS)]
    rows = []
    for ci in range(tm // A_CHUNK):
        r0 = ci * A_CHUNK
        rows.append(jnp.concatenate(
            [_dot(w_s[g], a_vn[r0:r0 + A_CHUNK, g * A_GROUP_DIM:(g + 1) * A_GROUP_DIM]) + abs_ref[g]
             for g in range(A_GROUPS)], axis=1))
    mixed = jnp.concatenate(rows, axis=0)
    y_a = a_u * mixed * _silu(proj(OFF_Z, D_MODEL))
    y_a = _dot(y_a.astype(BF16), wpa_ref[...])
    g_a = jax.nn.sigmoid(proj(OFF_MA, D_MODEL, w_mg_ref) + bm_ref[:, 0:D_MODEL])
    merged = g_a * y_a

    q_all = proj(OFF_Q, B_KEY_DIM) * (B_HEAD_K ** -0.5)
    k_all = proj(OFF_K, B_KEY_DIM)
    v_all = proj(OFF_BV, B_VAL_DIM).astype(BF16)
    g1 = proj(0, LANES, w_g1_ref)
    gk = _dot(g1.astype(BF16), wg2_ref[...]) + bbg_ref[...]
    log_a = (jnp.minimum(gk, 0.0) - jnp.log1p(jnp.exp(-jnp.abs(gk)))) * (1.0 / B_GATE_NORMALIZER)
    row = lax.broadcasted_iota(jnp.int32, (tm, 1), 0)
    col = lax.broadcasted_iota(jnp.int32, (1, tm), 1)
    tril_b = (lax.broadcasted_iota(jnp.int32, (tm, tm), 0)
              >= lax.broadcasted_iota(jnp.int32, (tm, tm), 1)).astype(BF16)
    la_hi, la_lo = _split_bf16(log_a)
    b_all = _dot(tril_b, la_hi) + _dot(tril_b, la_lo)

    heads = []
    for h in range(B_HEADS):
        ks = slice(h * B_HEAD_K, (h + 1) * B_HEAD_K)
        vs = slice(h * B_HEAD_V, (h + 1) * B_HEAD_V)
        o_h = _gla_head(q_all[:, ks], k_all[:, ks], v_all[:, vs], b_all[:, ks], state_ref, h, row, col)
        o_h = o_h * lax.rsqrt(jnp.mean(o_h * o_h, axis=-1, keepdims=True) + EPS) * bng_ref[...]
        heads.append(o_h)
    o_b = jnp.concatenate(heads, axis=1) * _silu(proj(OFF_BZ, B_VAL_DIM))
    y_b = _dot(o_b.astype(BF16), wpb_ref[...])
    g_b = jax.nn.sigmoid(proj(OFF_MB, D_MODEL, w_mg_ref) + bm_ref[:, D_MODEL:2 * D_MODEL])
    merged = merged + g_b * y_b

    out = _dot(merged.astype(BF16), wout_ref[...])
    hres = x + gate * out
    o_ref[0] = hres * lax.rsqrt(jnp.mean(hres * hres, axis=-1, keepdims=True) + EPS) * fg_ref[...]


def _const_spec(shape):
    nd = len(shape)
    return pl.BlockSpec(shape, lambda b, j: (0,) * nd, pipeline_mode=pl.Buffered(1))


def kernel(x, c, w_ada, b_ada, norm_g, w_in, a_ln_g, a_ln_b, a_w_s, a_b_s, b_w_gate2, b_b_gate, b_norm_g, w_proj_a, w_proj_b, b_merge, w_out, final_g):
    bsz, seq, d = x.shape
    assert d == D_MODEL and seq % TILE_M == 0 and w_ada.shape[0] == 1
    l = 0

    c_pad = jnp.zeros((SUBLANES, d), F32).at[:bsz].set(c)
    mod_cols = 1024
    mod = pl.pallas_call(
        _mod_kernel,
        grid=(3 * d // mod_cols,),
        in_specs=[pl.BlockSpec((SUBLANES, d), lambda n: (0, 0)),
                  pl.BlockSpec((d, mod_cols), lambda n: (0, n)),
                  pl.BlockSpec((1, mod_cols), lambda n: (0, n))],
        out_specs=pl.BlockSpec((SUBLANES, mod_cols), lambda n: (0, n)),
        out_shape=jax.ShapeDtypeStruct((SUBLANES, 3 * d), F32),
        name="adaln_mod",
    )(c_pad, w_ada[l], b_ada[l][None, :])
    mod = mod[:bsz, None, :]

    wl = w_in[l]
    w_main = wl[:, :W_MAIN].astype(BF16)
    w_merge = wl[:, W_MAIN + B_GATE_RANK:].astype(BF16)
    w_g1 = jnp.pad(wl[:, W_MAIN:W_MAIN + B_GATE_RANK], ((0, 0), (0, LANES - B_GATE_RANK))).astype(BF16)
    wg2 = jnp.zeros((LANES, B_KEY_DIM), F32).at[:B_GATE_RANK].set(b_w_gate2[l]).astype(BF16)
    abs_full = jnp.broadcast_to(a_b_s[l][:, :, None], (A_GROUPS, A_CHUNK, A_GROUP_DIM))

    row2 = lambda v: v.reshape(1, -1)
    operands = (
        x, mod, row2(norm_g[l]), w_main, w_merge, w_g1, row2(a_ln_g[l]), row2(a_ln_b[l]), a_w_s[l], abs_full,
        wg2, row2(b_b_gate[l]), row2(b_norm_g[l]), w_proj_a[l].astype(BF16), w_proj_b[l].astype(BF16),
        row2(b_merge[l]), w_out[l].astype(BF16), row2(final_g),
    )
    in_specs = [
        pl.BlockSpec((1, TILE_M, d), lambda b, j: (b, j, 0)),
        pl.BlockSpec((1, 1, 3 * d), lambda b, j: (b, 0, 0)),
    ] + [_const_spec(op.shape) for op in operands[2:]]

    return pl.pallas_call(
        _block_kernel,
        grid=(bsz, seq // TILE_M),
        in_specs=in_specs,
        out_specs=pl.BlockSpec((1, TILE_M, d), lambda b, j: (b, j, 0)),
        out_shape=jax.ShapeDtypeStruct((bsz, seq, d), x.dtype),
        scratch_shapes=[pltpu.VMEM((B_HEADS, B_HEAD_K, B_HEAD_V), F32)],
        compiler_params=pltpu.CompilerParams(
            dimension_semantics=("arbitrary", "arbitrary"),
            vmem_limit_bytes=VMEM_LIMIT_BYTES),
        name="hybrid_block",
    )(*operands)
```

```python
import math

import jax
import jax.numpy as jnp
from jax import lax
from jax.experimental import pallas as pl
from jax.experimental.pallas import tpu as pltpu

D_MODEL = 1024
A_GROUPS = 8
A_CHUNK = 128
A_GROUP_DIM = D_MODEL // A_GROUPS
B_HEADS = 4
B_KEY_DIM = D_MODEL // 2
B_VAL_DIM = D_MODEL
B_HEAD_K = B_KEY_DIM // B_HEADS
B_HEAD_V = B_VAL_DIM // B_HEADS
B_GATE_RANK = 16
B_GATE_NORMALIZER = 16.0
EPS = 1e-6

LANES = 128
SUBLANES = 8
TILE_M = 256
VMEM_LIMIT_BYTES = 56 * 1024 * 1024

OFF_U, OFF_V, OFF_Z = 0, 1024, 2048
OFF_Q, OFF_K = 3072, 3584
OFF_BV, OFF_BZ = 4096, 5120
W_MAIN = 6 * D_MODEL
OFF_MA, OFF_MB = 0, 1024

F32 = jnp.float32
BF16 = jnp.bfloat16
NEG_BIG = -1e30


def _dot(a, b):
    return jnp.dot(a, b, preferred_element_type=F32)


def _dot_nt(a, b):
    return lax.dot_general(a, b, (((1,), (1,)), ((), ())), preferred_element_type=F32)


def _dot_tn(a, b):
    return lax.dot_general(a, b, (((0,), (0,)), ((), ())), preferred_element_type=F32)


def _split_bf16(a):
    hi = a.astype(BF16)
    lo = (a - hi.astype(F32)).astype(BF16)
    return hi, lo


def _gelu(a):
    return 0.5 * a * (1.0 + lax.erf(a * (1.0 / math.sqrt(2.0))))


def _silu(a):
    return a * jax.nn.sigmoid(a)


def _rms(a):
    return a * lax.rsqrt(jnp.mean(a * a, axis=-1, keepdims=True) + EPS)


def _bcast_row_in_blocks(a, blk, row):
    m, n = a.shape
    a3 = a.reshape(m // blk, blk, n)
    return jnp.broadcast_to(a3[:, row:row + 1, :], (m // blk, blk, n)).reshape(m, n)


def _mod_kernel(c_ref, w_ref, b_ref, o_ref):
    c = c_ref[...]
    o_ref[...] = jnp.dot(_silu(c), w_ref[...], preferred_element_type=F32,
                         precision=lax.Precision.HIGHEST) + b_ref[...]


def _gla_masks(c):
    row = lax.broadcasted_iota(jnp.int32, (c, 1), 0)
    col = lax.broadcasted_iota(jnp.int32, (1, c), 1)
    siblings = []
    m = SUBLANES
    while m < c:
        siblings.append(((row // m) == (col // m) + 1) & ((col // m) % 2 == 0))
        m *= 2
    same_block = (row // SUBLANES) == (col // SUBLANES)
    expand = (lax.broadcasted_iota(jnp.int32, (LANES, c), 0)
              == lax.broadcasted_iota(jnp.int32, (LANES, c), 1) % SUBLANES).astype(BF16)
    return dict(row_in=row % SUBLANES, siblings=siblings, same_block=same_block, expand=expand)


def _gla_head(qs, k, v_b, b, state_ref, h, masks):
    c = qs.shape[0]
    state = state_ref[h]
    o = _dot((qs * jnp.exp(b)).astype(BF16), state.astype(BF16))

    scores = jnp.zeros((c, c), F32)
    m = SUBLANES
    for siblings in masks["siblings"]:
        ref_b = _bcast_row_in_blocks(b, 2 * m, m - 1)
        e = jnp.exp(-jnp.abs(b - ref_b))
        scores = jnp.where(siblings, _dot_nt((qs * e).astype(BF16), (k * e).astype(BF16)), scores)
        m *= 2

    lane = lax.broadcasted_iota(jnp.int32, (c, LANES), 1)
    diag = jnp.zeros((c, LANES), F32)
    for j in range(SUBLANES):
        b_j = _bcast_row_in_blocks(b, SUBLANES, j)
        k_j = _bcast_row_in_blocks(k, SUBLANES, j)
        e = jnp.exp(jnp.where(masks["row_in"] >= j, b - b_j, NEG_BIG))
        s_j = jnp.sum(qs * k_j * e, axis=-1, keepdims=True)
        diag = jnp.where(lane == j, s_j, diag)
    diag_full = _dot(diag.astype(BF16), masks["expand"])
    scores_b = jnp.where(masks["same_block"], diag_full.astype(BF16), scores.astype(BF16))
    o = o + _dot(scores_b, v_b)

    b_last = b[c - 1:c, :]
    k_dec = (k * jnp.exp(b_last - b)).astype(BF16)
    decay_col = jnp.transpose(jnp.broadcast_to(jnp.exp(b_last), (LANES, B_HEAD_K)))[:, 0:1]
    state_ref[h] = decay_col * state + _dot_tn(k_dec, v_b)
    return o


def _block_kernel(x_ref, mod_ref, ng_ref, w_in_ref, w_mg_ref, w_g1_ref, alng_ref, alnb_ref, aws_ref, abs_ref,
                  wg2_ref, bbg_ref, bng_ref, wpa_ref, wpb_ref, bm_ref, wout_ref, fg_ref,
                  o_ref, state_ref):
    @pl.when(pl.program_id(1) == 0)
    def _():
        state_ref[...] = jnp.zeros_like(state_ref)

    tm = x_ref.shape[1]
    x = x_ref[0]
    mod = mod_ref[0]
    shift = mod[:, 0:D_MODEL]
    scale = mod[:, D_MODEL:2 * D_MODEL]
    gate = mod[:, 2 * D_MODEL:3 * D_MODEL]

    hn_b = (_rms(x) * ng_ref[...] * (1.0 + scale) + shift).astype(BF16)

    def proj(lo, width, w_ref=w_in_ref):
        return _dot(hn_b, w_ref[:, lo:lo + width])


    q_all = proj(OFF_Q, B_KEY_DIM) * (B_HEAD_K ** -0.5)
    k_all = proj(OFF_K, B_KEY_DIM)
    v_all = proj(OFF_BV, B_VAL_DIM).astype(BF16)
    g1 = proj(0, LANES, w_g1_ref)
    gk = _dot(g1.astype(BF16), wg2_ref[:, 0:B_KEY_DIM]) + bbg_ref[...]
    log_a = (jnp.minimum(gk, 0.0) - jnp.log1p(jnp.exp(-jnp.abs(gk)))) * (1.0 / B_GATE_NORMALIZER)
    tril_b = (lax.broadcasted_iota(jnp.int32, (tm, tm), 0)
              >= lax.broadcasted_iota(jnp.int32, (tm, tm), 1)).astype(BF16)
    la_hi, la_lo = _split_bf16(log_a)
    b_all = _dot(tril_b, la_hi) + _dot(tril_b, la_lo)
    masks = _gla_masks(tm)

    def head(h):
        ks = slice(h * B_HEAD_K, (h + 1) * B_HEAD_K)
        vs = slice(h * B_HEAD_V, (h + 1) * B_HEAD_V)
        o_h = _gla_head(q_all[:, ks], k_all[:, ks], v_all[:, vs], b_all[:, ks], state_ref, h, masks)
        return _rms(o_h) * bng_ref[...]

    a_u = _gelu(proj(OFF_U, D_MODEL))
    o_0 = head(0)

    a_v = _gelu(proj(OFF_V, D_MODEL))
    a_vc = a_v - jnp.mean(a_v, axis=-1, keepdims=True)
    a_vn = (a_vc * lax.rsqrt(jnp.mean(a_vc * a_vc, axis=-1, keepdims=True) + EPS)
            * alng_ref[...] + alnb_ref[...]).astype(BF16)
    o_1 = head(1)

    tri = (lax.broadcasted_iota(jnp.int32, (A_CHUNK, A_CHUNK), 0)
           >= lax.broadcasted_iota(jnp.int32, (A_CHUNK, A_CHUNK), 1))
    w_s = [jnp.where(tri, aws_ref[g], 0.0).astype(BF16) for g in range(A_GROUPS)]
    rows = []
    for ci in range(tm // A_CHUNK):
        r0 = ci * A_CHUNK
        rows.append(jnp.concatenate(
            [_dot(w_s[g], a_vn[r0:r0 + A_CHUNK, g * A_GROUP_DIM:(g + 1) * A_GROUP_DIM]) + abs_ref[g]
             for g in range(A_GROUPS)], axis=1))
    mixed = jnp.concatenate(rows, axis=0)
    y_a = a_u * mixed * _silu(proj(OFF_Z, D_MODEL))
    o_2 = head(2)

    y_a = _dot(y_a.astype(BF16), wpa_ref[:, 0:D_MODEL])
    g_a = jax.nn.sigmoid(proj(OFF_MA, D_MODEL, w_mg_ref) + bm_ref[:, 0:D_MODEL])
    merged = g_a * y_a
    o_3 = head(3)

    b_z = _silu(proj(OFF_BZ, B_VAL_DIM))
    g_b = jax.nn.sigmoid(proj(OFF_MB, D_MODEL, w_mg_ref) + bm_ref[:, D_MODEL:2 * D_MODEL])
    o_b = jnp.concatenate([o_0, o_1, o_2, o_3], axis=1) * b_z
    y_b = _dot(o_b.astype(BF16), wpb_ref[:, 0:D_MODEL])
    merged = merged + g_b * y_b

    out = _dot(merged.astype(BF16), wout_ref[:, 0:D_MODEL])
    o_ref[0] = _rms(x + gate * out) * fg_ref[...]


def _bf16_weight(w):
    tiles = -(-w.shape[1] // LANES)
    tiles += 1 - tiles % 2
    return jnp.pad(w, ((0, 0), (0, tiles * LANES - w.shape[1]))).astype(BF16)


def _const_spec(shape):
    nd = len(shape)
    return pl.BlockSpec(shape, lambda b, j: (0,) * nd, pipeline_mode=pl.Buffered(1))


def kernel(x, c, w_ada, b_ada, norm_g, w_in, a_ln_g, a_ln_b, a_w_s, a_b_s, b_w_gate2, b_b_gate, b_norm_g, w_proj_a, w_proj_b, b_merge, w_out, final_g):
    bsz, seq, d = x.shape
    assert d == D_MODEL and seq % TILE_M == 0 and w_ada.shape[0] == 1
    l = 0

    c_pad = jnp.zeros((SUBLANES, d), F32).at[:bsz].set(c)
    mod_cols = 1024
    mod = pl.pallas_call(
        _mod_kernel,
        grid=(3 * d // mod_cols,),
        in_specs=[pl.BlockSpec((SUBLANES, d), lambda n: (0, 0)),
                  pl.BlockSpec((d, mod_cols), lambda n: (0, n)),
                  pl.BlockSpec((1, mod_cols), lambda n: (0, n))],
        out_specs=pl.BlockSpec((SUBLANES, mod_cols), lambda n: (0, n)),
        out_shape=jax.ShapeDtypeStruct((SUBLANES, 3 * d), F32),
        name="adaln_mod",
    )(c_pad, w_ada[l], b_ada[l][None, :])
    mod = mod[:bsz, None, :]

    wl = w_in[l]
    w_main = _bf16_weight(wl[:, :W_MAIN])
    w_merge = _bf16_weight(wl[:, W_MAIN + B_GATE_RANK:])
    w_g1 = _bf16_weight(wl[:, W_MAIN:W_MAIN + B_GATE_RANK])
    wg2 = _bf16_weight(jnp.pad(b_w_gate2[l], ((0, LANES - B_GATE_RANK), (0, 0))))
    abs_full = jnp.broadcast_to(a_b_s[l][:, :, None], (A_GROUPS, A_CHUNK, A_GROUP_DIM))

    row2 = lambda v: v.reshape(1, -1)
    operands = (
        x, mod, row2(norm_g[l]), w_main, w_merge, w_g1, row2(a_ln_g[l]), row2(a_ln_b[l]), a_w_s[l], abs_full,
        wg2, row2(b_b_gate[l]), row2(b_norm_g[l]), _bf16_weight(w_proj_a[l]), _bf16_weight(w_proj_b[l]),
        row2(b_merge[l]), _bf16_weight(w_out[l]), row2(final_g),
    )
    in_specs = [
        pl.BlockSpec((1, TILE_M, d), lambda b, j: (b, j, 0)),
        pl.BlockSpec((1, 1, 3 * d), lambda b, j: (b, 0, 0)),
    ] + [_const_spec(op.shape) for op in operands[2:]]

    return pl.pallas_call(
        _block_kernel,
        grid=(bsz, seq // TILE_M),
        in_specs=in_specs,
        out_specs=pl.BlockSpec((1, TILE_M, d), lambda b, j: (b, j, 0)),
        out_shape=jax.ShapeDtypeStruct((bsz, seq, d), x.dtype),
        scratch_shapes=[pltpu.VMEM((B_HEADS, B_HEAD_K, B_HEAD_V), F32)],
        compiler_params=pltpu.CompilerParams(
            dimension_semantics=("arbitrary", "arbitrary"),
            vmem_limit_bytes=VMEM_LIMIT_BYTES),
        name="hybrid_block",
    )(*operands)
```

```python
import math

import jax
import jax.numpy as jnp
from jax import lax
from jax.experimental import pallas as pl
from jax.experimental.pallas import tpu as pltpu

D_MODEL = 1024
A_GROUPS = 8
A_CHUNK = 128
A_GROUP_DIM = D_MODEL // A_GROUPS
B_HEADS = 4
B_KEY_DIM = D_MODEL // 2
B_VAL_DIM = D_MODEL
B_HEAD_K = B_KEY_DIM // B_HEADS
B_HEAD_V = B_VAL_DIM // B_HEADS
B_GATE_RANK = 16
B_GATE_NORMALIZER = 16.0
EPS = 1e-6

LANES = 128
SUBLANES = 8
TILE_M = 256
VMEM_LIMIT_BYTES = 56 * 1024 * 1024

OFF_U, OFF_V, OFF_Z = 0, 1024, 2048
OFF_Q, OFF_K = 3072, 3584
OFF_BV, OFF_BZ = 4096, 5120
W_MAIN = 6 * D_MODEL
OFF_MA, OFF_MB = 0, 1024

F32 = jnp.float32
BF16 = jnp.bfloat16
LOG2E = 1.4426950408889634
FAST_SPAN_LOG2 = 80.0


def _dot(a, b):
    return jnp.dot(a, b, preferred_element_type=F32)


def _dot_nt(a, b):
    return lax.dot_general(a, b, (((1,), (1,)), ((), ())), preferred_element_type=F32)


def _dot_tn(a, b):
    return lax.dot_general(a, b, (((0,), (0,)), ((), ())), preferred_element_type=F32)


def _split_bf16(a):
    hi = a.astype(BF16)
    lo = (a - hi.astype(F32)).astype(BF16)
    return hi, lo


def _gelu(a):
    return 0.5 * a * (1.0 + lax.erf(a * (1.0 / math.sqrt(2.0))))


def _silu(a):
    return a * jax.nn.sigmoid(a)


def _rms(a):
    return a * lax.rsqrt(jnp.mean(a * a, axis=-1, keepdims=True) + EPS)


def _bcast_row_in_blocks(a, blk, row):
    m, n = a.shape
    a3 = a.reshape(m // blk, blk, n)
    return jnp.broadcast_to(a3[:, row:row + 1, :], (m // blk, blk, n)).reshape(m, n)


def _mod_kernel(c_ref, w_ref, b_ref, o_ref):
    c = c_ref[...]
    o_ref[...] = jnp.dot(_silu(c), w_ref[...], preferred_element_type=F32,
                         precision=lax.Precision.HIGHEST) + b_ref[...]


def _scores_single_reference(qs, k, b):
    c = qs.shape[0]
    ref_b = b[c // 2 - 1:c // 2, :]
    s = _dot_nt((qs * jnp.exp2(b - ref_b)).astype(BF16), (k * jnp.exp2(ref_b - b)).astype(BF16))
    causal = (lax.broadcasted_iota(jnp.int32, (c, c), 0) >= lax.broadcasted_iota(jnp.int32, (c, c), 1))
    return jnp.where(causal, s, 0.0).astype(BF16)


def _scores_hierarchical(qs, k, b):
    c = qs.shape[0]
    row = lax.broadcasted_iota(jnp.int32, (c, 1), 0)
    col = lax.broadcasted_iota(jnp.int32, (1, c), 1)
    scores = jnp.zeros((c, c), F32)
    m = SUBLANES
    while m < c:
        ref_b = _bcast_row_in_blocks(b, 2 * m, m - 1)
        e = jnp.exp2(-jnp.abs(b - ref_b))
        siblings = ((row // m) == (col // m) + 1) & ((col // m) % 2 == 0)
        scores = jnp.where(siblings, _dot_nt((qs * e).astype(BF16), (k * e).astype(BF16)), scores)
        m *= 2
    row_in = row % SUBLANES
    lane = lax.broadcasted_iota(jnp.int32, (1, LANES), 1)
    diag = jnp.zeros((c, LANES), F32)
    for j in range(SUBLANES):
        b_j = _bcast_row_in_blocks(b, SUBLANES, j)
        k_j = _bcast_row_in_blocks(k, SUBLANES, j)
        s_j = jnp.sum(qs * k_j * jnp.exp2(jnp.minimum(b - b_j, 0.0)), axis=-1, keepdims=True)
        diag = jnp.where((lane == j) & (row_in >= j), s_j, diag)
    expand = (lax.broadcasted_iota(jnp.int32, (LANES, c), 0)
              == lax.broadcasted_iota(jnp.int32, (LANES, c), 1) % SUBLANES).astype(BF16)
    diag_full = _dot(diag.astype(BF16), expand)
    same_block = (row // SUBLANES) == (col // SUBLANES)
    return jnp.where(same_block, diag_full.astype(BF16), scores.astype(BF16))


def _gla_head(qs, k, v_b, b, scores_b, state_ref, h):
    c = qs.shape[0]
    state = state_ref[h]
    o = _dot((qs * jnp.exp2(b)).astype(BF16), state.astype(BF16)) + _dot(scores_b, v_b)
    b_last = b[c - 1:c, :]
    k_dec = (k * jnp.exp2(b_last - b)).astype(BF16)
    decay_col = jnp.transpose(jnp.broadcast_to(jnp.exp2(b_last), (LANES, B_HEAD_K)))[:, 0:1]
    state_ref[h] = decay_col * state + _dot_tn(k_dec, v_b)
    return o


def _block_kernel(x_ref, mod_ref, ng_ref, w_in_ref, w_mg_ref, w_g1_ref, alng_ref, alnb_ref, aws_ref, abs_ref,
                  wg2_ref, bbg_ref, bng_ref, wpa_ref, wpb_ref, bm_ref, wout_ref, fg_ref,
                  o_ref, state_ref, scores_ref):
    @pl.when(pl.program_id(1) == 0)
    def _():
        state_ref[...] = jnp.zeros_like(state_ref)

    tm = x_ref.shape[1]
    x = x_ref[0]
    mod = mod_ref[0]
    shift = mod[:, 0:D_MODEL]
    scale = mod[:, D_MODEL:2 * D_MODEL]
    gate = mod[:, 2 * D_MODEL:3 * D_MODEL]

    hn_b = (_rms(x) * ng_ref[...] * (1.0 + scale) + shift).astype(BF16)

    def proj(lo, width, w_ref=w_in_ref):
        return _dot(hn_b, w_ref[:, lo:lo + width])

    g1 = proj(0, LANES, w_g1_ref)
    gk = _dot(g1.astype(BF16), wg2_ref[:, 0:B_KEY_DIM]) + bbg_ref[...]
    log2_a = (jnp.minimum(gk, 0.0) - jnp.log1p(jnp.exp(-jnp.abs(gk)))) * (LOG2E / B_GATE_NORMALIZER)
    tril_b = (lax.broadcasted_iota(jnp.int32, (tm, tm), 0)
              >= lax.broadcasted_iota(jnp.int32, (tm, tm), 1)).astype(BF16)
    la_hi, la_lo = _split_bf16(log2_a)
    b_all = _dot(tril_b, la_hi) + _dot(tril_b, la_lo)
    q_all = proj(OFF_Q, B_KEY_DIM) * (B_HEAD_K ** -0.5)
    k_all = proj(OFF_K, B_KEY_DIM)
    v_all = proj(OFF_BV, B_VAL_DIM).astype(BF16)

    head_args = []
    for h in range(B_HEADS):
        ks = slice(h * B_HEAD_K, (h + 1) * B_HEAD_K)
        head_args.append((q_all[:, ks], k_all[:, ks], b_all[:, ks]))

    a_u = _gelu(proj(OFF_U, D_MODEL))
    a_v = _gelu(proj(OFF_V, D_MODEL))
    a_vc = a_v - jnp.mean(a_v, axis=-1, keepdims=True)
    a_vn = (a_vc * lax.rsqrt(jnp.mean(a_vc * a_vc, axis=-1, keepdims=True) + EPS)
            * alng_ref[...] + alnb_ref[...]).astype(BF16)

    tri = (lax.broadcasted_iota(jnp.int32, (A_CHUNK, A_CHUNK), 0)
           >= lax.broadcasted_iota(jnp.int32, (A_CHUNK, A_CHUNK), 1))
    w_s = [jnp.where(tri, aws_ref[g], 0.0).astype(BF16) for g in range(A_GROUPS)]
    rows = []
    for ci in range(tm // A_CHUNK):
        r0 = ci * A_CHUNK
        rows.append(jnp.concatenate(
            [_dot(w_s[g], a_vn[r0:r0 + A_CHUNK, g * A_GROUP_DIM:(g + 1) * A_GROUP_DIM]) + abs_ref[g]
             for g in range(A_GROUPS)], axis=1))
    mixed = jnp.concatenate(rows, axis=0)
    y_a = a_u * mixed * _silu(proj(OFF_Z, D_MODEL))
    y_a = _dot(y_a.astype(BF16), wpa_ref[:, 0:D_MODEL])
    g_a = jax.nn.sigmoid(proj(OFF_MA, D_MODEL, w_mg_ref) + bm_ref[:, 0:D_MODEL])
    merged = g_a * y_a
    b_z = _silu(proj(OFF_BZ, B_VAL_DIM))
    g_b = jax.nn.sigmoid(proj(OFF_MB, D_MODEL, w_mg_ref) + bm_ref[:, D_MODEL:2 * D_MODEL])

    for h in range(B_HEADS):
        scores_ref[h] = _scores_single_reference(*head_args[h])
    b_mid = b_all[tm // 2 - 1:tm // 2, :]
    b_last = b_all[tm - 1:tm, :]
    wide_span = jnp.max(jnp.maximum(-b_mid, b_mid - b_last)) >= FAST_SPAN_LOG2

    @pl.when(wide_span)
    def _():
        for h in range(B_HEADS):
            scores_ref[h] = _scores_hierarchical(*head_args[h])

    heads = []
    for h in range(B_HEADS):
        qs, k, b = head_args[h]
        o_h = _gla_head(qs, k, v_all[:, h * B_HEAD_V:(h + 1) * B_HEAD_V], b, scores_ref[h], state_ref, h)
        heads.append(_rms(o_h) * bng_ref[...])
    o_b = jnp.concatenate(heads, axis=1) * b_z
    y_b = _dot(o_b.astype(BF16), wpb_ref[:, 0:D_MODEL])
    merged = merged + g_b * y_b

    out = _dot(merged.astype(BF16), wout_ref[:, 0:D_MODEL])
    o_ref[0] = _rms(x + gate * out) * fg_ref[...]


def _bf16_weight(w):
    tiles = -(-w.shape[1] // LANES)
    tiles += 1 - tiles % 2
    return jnp.pad(w, ((0, 0), (0, tiles * LANES - w.shape[1]))).astype(BF16)


def _const_spec(shape):
    nd = len(shape)
    return pl.BlockSpec(shape, lambda b, j: (0,) * nd, pipeline_mode=pl.Buffered(1))


def kernel(x, c, w_ada, b_ada, norm_g, w_in, a_ln_g, a_ln_b, a_w_s, a_b_s, b_w_gate2, b_b_gate, b_norm_g, w_proj_a, w_proj_b, b_merge, w_out, final_g):
    bsz, seq, d = x.shape
    assert d == D_MODEL and seq % TILE_M == 0 and w_ada.shape[0] == 1
    l = 0

    c_pad = jnp.zeros((SUBLANES, d), F32).at[:bsz].set(c)
    mod_cols = 1024
    mod = pl.pallas_call(
        _mod_kernel,
        grid=(3 * d // mod_cols,),
        in_specs=[pl.BlockSpec((SUBLANES, d), lambda n: (0, 0)),
                  pl.BlockSpec((d, mod_cols), lambda n: (0, n)),
                  pl.BlockSpec((1, mod_cols), lambda n: (0, n))],
        out_specs=pl.BlockSpec((SUBLANES, mod_cols), lambda n: (0, n)),
        out_shape=jax.ShapeDtypeStruct((SUBLANES, 3 * d), F32),
        name="adaln_mod",
    )(c_pad, w_ada[l], b_ada[l][None, :])
    mod = mod[:bsz, None, :]

    wl = w_in[l]
    w_main = _bf16_weight(wl[:, :W_MAIN])
    w_merge = _bf16_weight(wl[:, W_MAIN + B_GATE_RANK:])
    w_g1 = _bf16_weight(wl[:, W_MAIN:W_MAIN + B_GATE_RANK])
    wg2 = _bf16_weight(jnp.pad(b_w_gate2[l], ((0, LANES - B_GATE_RANK), (0, 0))))
    abs_full = jnp.broadcast_to(a_b_s[l][:, :, None], (A_GROUPS, A_CHUNK, A_GROUP_DIM))

    row2 = lambda v: v.reshape(1, -1)
    operands = (
        x, mod, row2(norm_g[l]), w_main, w_merge, w_g1, row2(a_ln_g[l]), row2(a_ln_b[l]), a_w_s[l], abs_full,
        wg2, row2(b_b_gate[l]), row2(b_norm_g[l]), _bf16_weight(w_proj_a[l]), _bf16_weight(w_proj_b[l]),
        row2(b_merge[l]), _bf16_weight(w_out[l]), row2(final_g),
    )
    in_specs = [
        pl.BlockSpec((1, TILE_M, d), lambda b, j: (b, j, 0)),
        pl.BlockSpec((1, 1, 3 * d), lambda b, j: (b, 0, 0)),
    ] + [_const_spec(op.shape) for op in operands[2:]]

    return pl.pallas_call(
        _block_kernel,
        grid=(bsz, seq // TILE_M),
        in_specs=in_specs,
        out_specs=pl.BlockSpec((1, TILE_M, d), lambda b, j: (b, j, 0)),
        out_shape=jax.ShapeDtypeStruct((bsz, seq, d), x.dtype),
        scratch_shapes=[pltpu.VMEM((B_HEADS, B_HEAD_K, B_HEAD_V), F32),
                        pltpu.VMEM((B_HEADS, TILE_M, TILE_M), BF16)],
        compiler_params=pltpu.CompilerParams(
            dimension_semantics=("arbitrary", "arbitrary"),
            vmem_limit_bytes=VMEM_LIMIT_BYTES),
        name="hybrid_block",
    )(*operands)
```

```python
import math

import jax
import jax.numpy as jnp
from jax import lax
from jax.experimental import pallas as pl
from jax.experimental.pallas import tpu as pltpu

D_MODEL = 1024
A_GROUPS = 8
A_CHUNK = 128
A_GROUP_DIM = D_MODEL // A_GROUPS
B_HEADS = 4
B_KEY_DIM = D_MODEL // 2
B_VAL_DIM = D_MODEL
B_HEAD_K = B_KEY_DIM // B_HEADS
B_HEAD_V = B_VAL_DIM // B_HEADS
B_GATE_RANK = 16
B_GATE_NORMALIZER = 16.0
EPS = 1e-6

LANES = 128
SUBLANES = 8
TILE_M = 256
VMEM_LIMIT_BYTES = 56 * 1024 * 1024

OFF_U, OFF_V, OFF_Z = 0, 1024, 2048
OFF_Q, OFF_K = 3072, 3584
OFF_BV, OFF_BZ = 4096, 5120
W_MAIN = 6 * D_MODEL
OFF_MA, OFF_MB = 0, 1024

F32 = jnp.float32
BF16 = jnp.bfloat16
LOG2E = 1.4426950408889634
FAST_SPAN_LOG2 = 80.0


def _dot(a, b):
    return jnp.dot(a, b, preferred_element_type=F32)


def _dot_nt(a, b):
    return lax.dot_general(a, b, (((1,), (1,)), ((), ())), preferred_element_type=F32)


def _dot_tn(a, b):
    return lax.dot_general(a, b, (((0,), (0,)), ((), ())), preferred_element_type=F32)


def _split_bf16(a):
    hi = a.astype(BF16)
    lo = (a - hi.astype(F32)).astype(BF16)
    return hi, lo


def _gelu(a):
    return 0.5 * a * (1.0 + lax.erf(a * (1.0 / math.sqrt(2.0))))


def _silu(a):
    return a * jax.nn.sigmoid(a)


def _rms(a):
    return a * lax.rsqrt(jnp.mean(a * a, axis=-1, keepdims=True) + EPS)


def _bcast_row_in_blocks(a, blk, row):
    m, n = a.shape
    a3 = a.reshape(m // blk, blk, n)
    return jnp.broadcast_to(a3[:, row:row + 1, :], (m // blk, blk, n)).reshape(m, n)


def _mod_kernel(c_ref, w_ref, b_ref, o_ref):
    c = c_ref[...]
    o_ref[...] = jnp.dot(_silu(c), w_ref[...], preferred_element_type=F32,
                         precision=lax.Precision.HIGHEST) + b_ref[...]


def _scores_single_reference(qs, k, b):
    c = qs.shape[0]
    ref_b = b[c // 2 - 1:c // 2, :]
    s = _dot_nt((qs * jnp.exp2(b - ref_b)).astype(BF16), (k * jnp.exp2(ref_b - b)).astype(BF16))
    causal = (lax.broadcasted_iota(jnp.int32, (c, c), 0) >= lax.broadcasted_iota(jnp.int32, (c, c), 1))
    return jnp.where(causal, s, 0.0).astype(BF16)


def _scores_hierarchical(qs, k, b):
    c = qs.shape[0]
    row = lax.broadcasted_iota(jnp.int32, (c, 1), 0)
    col = lax.broadcasted_iota(jnp.int32, (1, c), 1)
    scores = jnp.zeros((c, c), F32)
    m = SUBLANES
    while m < c:
        ref_b = _bcast_row_in_blocks(b, 2 * m, m - 1)
        e = jnp.exp2(-jnp.abs(b - ref_b))
        siblings = ((row // m) == (col // m) + 1) & ((col // m) % 2 == 0)
        scores = jnp.where(siblings, _dot_nt((qs * e).astype(BF16), (k * e).astype(BF16)), scores)
        m *= 2
    row_in = row % SUBLANES
    lane = lax.broadcasted_iota(jnp.int32, (1, LANES), 1)
    diag = jnp.zeros((c, LANES), F32)
    for j in range(SUBLANES):
        b_j = _bcast_row_in_blocks(b, SUBLANES, j)
        k_j = _bcast_row_in_blocks(k, SUBLANES, j)
        s_j = jnp.sum(qs * k_j * jnp.exp2(jnp.minimum(b - b_j, 0.0)), axis=-1, keepdims=True)
        diag = jnp.where((lane == j) & (row_in >= j), s_j, diag)
    expand = (lax.broadcasted_iota(jnp.int32, (LANES, c), 0)
              == lax.broadcasted_iota(jnp.int32, (LANES, c), 1) % SUBLANES).astype(BF16)
    diag_full = _dot(diag.astype(BF16), expand)
    same_block = (row // SUBLANES) == (col // SUBLANES)
    return jnp.where(same_block, diag_full.astype(BF16), scores.astype(BF16))


def _gla_inter_chunk(qs, k, v_b, b, state_ref, h):
    c = qs.shape[0]
    state = state_ref[h]
    o = _dot((qs * jnp.exp2(b)).astype(BF16), state.astype(BF16))
    b_last = b[c - 1:c, :]
    k_dec = (k * jnp.exp2(b_last - b)).astype(BF16)
    decay_col = jnp.transpose(jnp.broadcast_to(jnp.exp2(b_last), (LANES, B_HEAD_K)))[:, 0:1]
    state_ref[h] = decay_col * state + _dot_tn(k_dec, v_b)
    return o


def _block_kernel(x_ref, mod_ref, ng_ref, w_in_ref, w_mg_ref, w_g1_ref, alng_ref, alnb_ref, aws_ref, abs_ref,
                  wg2_ref, bbg_ref, bng_ref, wpa_ref, wpb_ref, bm_ref, wout_ref, fg_ref,
                  o_ref, state_ref, scores_ref):
    @pl.when(pl.program_id(1) == 0)
    def _():
        state_ref[...] = jnp.zeros_like(state_ref)

    tm = x_ref.shape[1]
    x = x_ref[0]
    mod = mod_ref[0]
    shift = mod[:, 0:D_MODEL]
    scale = mod[:, D_MODEL:2 * D_MODEL]
    gate = mod[:, 2 * D_MODEL:3 * D_MODEL]

    hn_b = (_rms(x) * ng_ref[...] * (1.0 + scale) + shift).astype(BF16)

    def proj(lo, width, w_ref=w_in_ref):
        return _dot(hn_b, w_ref[:, lo:lo + width])

    g1 = proj(0, LANES, w_g1_ref)
    q_all = proj(OFF_Q, B_KEY_DIM) * (B_HEAD_K ** -0.5)
    k_all = proj(OFF_K, B_KEY_DIM)
    gk = _dot(g1.astype(BF16), wg2_ref[:, 0:B_KEY_DIM]) + bbg_ref[...]
    p_bv = proj(OFF_BV, B_VAL_DIM)
    log2_a = (jnp.minimum(gk, 0.0) - jnp.log1p(jnp.exp(-jnp.abs(gk)))) * (LOG2E / B_GATE_NORMALIZER)
    tril_b = (lax.broadcasted_iota(jnp.int32, (tm, tm), 0)
              >= lax.broadcasted_iota(jnp.int32, (tm, tm), 1)).astype(BF16)
    la_hi, la_lo = _split_bf16(log2_a)
    p_u = proj(OFF_U, D_MODEL)
    b_all = _dot(tril_b, la_hi) + _dot(tril_b, la_lo)
    v_all = p_bv.astype(BF16)
    p_v = proj(OFF_V, D_MODEL)

    head_args = []
    for h in range(B_HEADS):
        ks = slice(h * B_HEAD_K, (h + 1) * B_HEAD_K)
        head_args.append((q_all[:, ks], k_all[:, ks], b_all[:, ks]))

    p_z = proj(OFF_Z, D_MODEL)
    a_u = _gelu(p_u)
    a_v = _gelu(p_v)
    p_ma = proj(OFF_MA, D_MODEL, w_mg_ref)
    a_vc = a_v - jnp.mean(a_v, axis=-1, keepdims=True)
    a_vn = (a_vc * lax.rsqrt(jnp.mean(a_vc * a_vc, axis=-1, keepdims=True) + EPS)
            * alng_ref[...] + alnb_ref[...]).astype(BF16)

    o_inter = [_gla_inter_chunk(*head_args[h][:2], v_all[:, h * B_HEAD_V:(h + 1) * B_HEAD_V],
                                head_args[h][2], state_ref, h) for h in range(B_HEADS)]

    tri = (lax.broadcasted_iota(jnp.int32, (A_CHUNK, A_CHUNK), 0)
           >= lax.broadcasted_iota(jnp.int32, (A_CHUNK, A_CHUNK), 1))
    n_chunks = tm // A_CHUNK
    mixed_g = []
    for g in range(A_GROUPS):
        cols = slice(g * A_GROUP_DIM, (g + 1) * A_GROUP_DIM)
        v_g = jnp.concatenate([a_vn[ci * A_CHUNK:(ci + 1) * A_CHUNK, cols] for ci in range(n_chunks)], axis=1)
        m_g = _dot(jnp.where(tri, aws_ref[g], 0.0).astype(BF16), v_g)
        mixed_g.append(jnp.concatenate(
            [m_g[:, ci * A_GROUP_DIM:(ci + 1) * A_GROUP_DIM] for ci in range(n_chunks)], axis=0) + jnp.concatenate(
            [abs_ref[g]] * n_chunks, axis=0))
    mixed = jnp.concatenate(mixed_g, axis=1)
    y_a = a_u * mixed * _silu(p_z)
    g_a = jax.nn.sigmoid(p_ma + bm_ref[:, 0:D_MODEL])
    merged = g_a * _dot(y_a.astype(BF16), wpa_ref[:, 0:D_MODEL])

    for h in range(B_HEADS):
        scores_ref[h] = _scores_single_reference(*head_args[h])
    b_mid = b_all[tm // 2 - 1:tm // 2, :]
    b_last = b_all[tm - 1:tm, :]
    wide_span = jnp.max(jnp.maximum(-b_mid, b_mid - b_last)) >= FAST_SPAN_LOG2

    @pl.when(wide_span)
    def _():
        for h in range(B_HEADS):
            scores_ref[h] = _scores_hierarchical(*head_args[h])

    p_bz = proj(OFF_BZ, B_VAL_DIM)
    heads = []
    for h in range(B_HEADS):
        o_h = o_inter[h] + _dot(scores_ref[h], v_all[:, h * B_HEAD_V:(h + 1) * B_HEAD_V])
        heads.append(_rms(o_h) * bng_ref[...])
    p_mb = proj(OFF_MB, D_MODEL, w_mg_ref)
    o_b = jnp.concatenate(heads, axis=1) * _silu(p_bz)
    y_b = _dot(o_b.astype(BF16), wpb_ref[:, 0:D_MODEL])
    g_b = jax.nn.sigmoid(p_mb + bm_ref[:, D_MODEL:2 * D_MODEL])
    merged = merged + g_b * y_b

    out = _dot(merged.astype(BF16), wout_ref[:, 0:D_MODEL])
    o_ref[0] = _rms(x + gate * out) * fg_ref[...]


def _bf16_weight(w):
    tiles = -(-w.shape[1] // LANES)
    tiles += 1 - tiles % 2
    return jnp.pad(w, ((0, 0), (0, tiles * LANES - w.shape[1]))).astype(BF16)


def _const_spec(shape):
    nd = len(shape)
    return pl.BlockSpec(shape, lambda b, j: (0,) * nd, pipeline_mode=pl.Buffered(1))


def kernel(x, c, w_ada, b_ada, norm_g, w_in, a_ln_g, a_ln_b, a_w_s, a_b_s, b_w_gate2, b_b_gate, b_norm_g, w_proj_a, w_proj_b, b_merge, w_out, final_g):
    bsz, seq, d = x.shape
    assert d == D_MODEL and seq % TILE_M == 0 and w_ada.shape[0] == 1
    l = 0

    c_pad = jnp.zeros((SUBLANES, d), F32).at[:bsz].set(c)
    mod_cols = 1024
    mod = pl.pallas_call(
        _mod_kernel,
        grid=(3 * d // mod_cols,),
        in_specs=[pl.BlockSpec((SUBLANES, d), lambda n: (0, 0)),
                  pl.BlockSpec((d, mod_cols), lambda n: (0, n)),
                  pl.BlockSpec((1, mod_cols), lambda n: (0, n))],
        out_specs=pl.BlockSpec((SUBLANES, mod_cols), lambda n: (0, n)),
        out_shape=jax.ShapeDtypeStruct((SUBLANES, 3 * d), F32),
        name="adaln_mod",
    )(c_pad, w_ada[l], b_ada[l][None, :])
    mod = mod[:bsz, None, :]

    wl = w_in[l]
    w_main = _bf16_weight(wl[:, :W_MAIN])
    w_merge = _bf16_weight(wl[:, W_MAIN + B_GATE_RANK:])
    w_g1 = _bf16_weight(wl[:, W_MAIN:W_MAIN + B_GATE_RANK])
    wg2 = _bf16_weight(jnp.pad(b_w_gate2[l], ((0, LANES - B_GATE_RANK), (0, 0))))
    abs_full = jnp.broadcast_to(a_b_s[l][:, :, None], (A_GROUPS, A_CHUNK, A_GROUP_DIM))

    row2 = lambda v: v.reshape(1, -1)
    operands = (
        x, mod, row2(norm_g[l]), w_main, w_merge, w_g1, row2(a_ln_g[l]), row2(a_ln_b[l]), a_w_s[l], abs_full,
        wg2, row2(b_b_gate[l]), row2(b_norm_g[l]), _bf16_weight(w_proj_a[l]), _bf16_weight(w_proj_b[l]),
        row2(b_merge[l]), _bf16_weight(w_out[l]), row2(final_g),
    )
    in_specs = [
        pl.BlockSpec((1, TILE_M, d), lambda b, j: (b, j, 0)),
        pl.BlockSpec((1, 1, 3 * d), lambda b, j: (b, 0, 0)),
    ] + [_const_spec(op.shape) for op in operands[2:]]

    return pl.pallas_call(
        _block_kernel,
        grid=(bsz, seq // TILE_M),
        in_specs=in_specs,
        out_specs=pl.BlockSpec((1, TILE_M, d), lambda b, j: (b, j, 0)),
        out_shape=jax.ShapeDtypeStruct((bsz, seq, d), x.dtype),
        scratch_shapes=[pltpu.VMEM((B_HEADS, B_HEAD_K, B_HEAD_V), F32),
                        pltpu.VMEM((B_HEADS, TILE_M, TILE_M), BF16)],
        compiler_params=pltpu.CompilerParams(
            dimension_semantics=("arbitrary", "arbitrary"),
            vmem_limit_bytes=VMEM_LIMIT_BYTES),
        name="hybrid_block",
    )(*operands)
```

```python
import math

import jax
import jax.numpy as jnp
from jax import lax
from jax.experimental import pallas as pl
from jax.experimental.pallas import tpu as pltpu

D_MODEL = 1024
A_GROUPS = 8
A_CHUNK = 128
A_GROUP_DIM = D_MODEL // A_GROUPS
B_HEADS = 4
B_KEY_DIM = D_MODEL // 2
B_VAL_DIM = D_MODEL
B_HEAD_K = B_KEY_DIM // B_HEADS
B_HEAD_V = B_VAL_DIM // B_HEADS
B_GATE_RANK = 16
B_GATE_NORMALIZER = 16.0
EPS = 1e-6

LANES = 128
SUBLANES = 8
GLA_CHUNK = 256
TILE_M = 512
VMEM_LIMIT_BYTES = 56 * 1024 * 1024

OFF_U, OFF_V, OFF_Z = 0, 1024, 2048
OFF_Q, OFF_K = 3072, 3584
OFF_BV, OFF_BZ = 4096, 5120
W_MAIN = 6 * D_MODEL
OFF_MA, OFF_MB = 0, 1024

F32 = jnp.float32
BF16 = jnp.bfloat16
LOG2E = 1.4426950408889634
FAST_SPAN_LOG2 = 80.0


def _dot(a, b):
    return jnp.dot(a, b, preferred_element_type=F32)


def _dot_nt(a, b):
    return lax.dot_general(a, b, (((1,), (1,)), ((), ())), preferred_element_type=F32)


def _dot_tn(a, b):
    return lax.dot_general(a, b, (((0,), (0,)), ((), ())), preferred_element_type=F32)


def _split_bf16(a):
    hi = a.astype(BF16)
    lo = (a - hi.astype(F32)).astype(BF16)
    return hi, lo


def _gelu(a):
    return 0.5 * a * (1.0 + lax.erf(a * (1.0 / math.sqrt(2.0))))


def _silu(a):
    return a * jax.nn.sigmoid(a)


def _rms(a):
    return a * lax.rsqrt(jnp.mean(a * a, axis=-1, keepdims=True) + EPS)


def _bcast_row_in_blocks(a, blk, row):
    m, n = a.shape
    a3 = a.reshape(m // blk, blk, n)
    return jnp.broadcast_to(a3[:, row:row + 1, :], (m // blk, blk, n)).reshape(m, n)


def _mod_kernel(c_ref, w_ref, b_ref, o_ref):
    c = c_ref[...]
    o_ref[...] = jnp.dot(_silu(c), w_ref[...], preferred_element_type=F32,
                         precision=lax.Precision.HIGHEST) + b_ref[...]


def _scores_single_reference(qs, k, b):
    c = qs.shape[0]
    ref_b = b[c // 2 - 1:c // 2, :]
    s = _dot_nt((qs * jnp.exp2(b - ref_b)).astype(BF16), (k * jnp.exp2(ref_b - b)).astype(BF16))
    causal = (lax.broadcasted_iota(jnp.int32, (c, c), 0) >= lax.broadcasted_iota(jnp.int32, (c, c), 1))
    return jnp.where(causal, s, 0.0).astype(BF16)


def _scores_hierarchical(qs, k, b):
    c = qs.shape[0]
    row = lax.broadcasted_iota(jnp.int32, (c, 1), 0)
    col = lax.broadcasted_iota(jnp.int32, (1, c), 1)
    scores = jnp.zeros((c, c), F32)
    m = SUBLANES
    while m < c:
        ref_b = _bcast_row_in_blocks(b, 2 * m, m - 1)
        e = jnp.exp2(-jnp.abs(b - ref_b))
        siblings = ((row // m) == (col // m) + 1) & ((col // m) % 2 == 0)
        scores = jnp.where(siblings, _dot_nt((qs * e).astype(BF16), (k * e).astype(BF16)), scores)
        m *= 2
    row_in = row % SUBLANES
    lane = lax.broadcasted_iota(jnp.int32, (1, LANES), 1)
    diag = jnp.zeros((c, LANES), F32)
    for j in range(SUBLANES):
        b_j = _bcast_row_in_blocks(b, SUBLANES, j)
        k_j = _bcast_row_in_blocks(k, SUBLANES, j)
        s_j = jnp.sum(qs * k_j * jnp.exp2(jnp.minimum(b - b_j, 0.0)), axis=-1, keepdims=True)
        diag = jnp.where((lane == j) & (row_in >= j), s_j, diag)
    expand = (lax.broadcasted_iota(jnp.int32, (LANES, c), 0)
              == lax.broadcasted_iota(jnp.int32, (LANES, c), 1) % SUBLANES).astype(BF16)
    diag_full = _dot(diag.astype(BF16), expand)
    same_block = (row // SUBLANES) == (col // SUBLANES)
    return jnp.where(same_block, diag_full.astype(BF16), scores.astype(BF16))


def _gla_inter_chunk(qs, k, v_b, b, state_ref, h):
    c = qs.shape[0]
    state = state_ref[h]
    o = _dot((qs * jnp.exp2(b)).astype(BF16), state.astype(BF16))
    b_last = b[c - 1:c, :]
    k_dec = (k * jnp.exp2(b_last - b)).astype(BF16)
    decay_col = jnp.transpose(jnp.broadcast_to(jnp.exp2(b_last), (LANES, B_HEAD_K)))[:, 0:1]
    state_ref[h] = decay_col * state + _dot_tn(k_dec, v_b)
    return o


def _block_kernel(x_ref, mod_ref, ng_ref, w_in_ref, w_mg_ref, w_g1_ref, alng_ref, alnb_ref, aws_ref, abs_ref,
                  wg2_ref, bbg_ref, bng_ref, wpa_ref, wpb_ref, bm_ref, wout_ref, fg_ref,
                  o_ref, state_ref, scores_ref):
    @pl.when(pl.program_id(1) == 0)
    def _():
        state_ref[...] = jnp.zeros_like(state_ref)

    tm = x_ref.shape[1]
    x = x_ref[0]
    mod = mod_ref[0]
    shift = mod[:, 0:D_MODEL]
    scale = mod[:, D_MODEL:2 * D_MODEL]
    gate = mod[:, 2 * D_MODEL:3 * D_MODEL]

    hn_b = (_rms(x) * ng_ref[...] * (1.0 + scale) + shift).astype(BF16)

    def proj(lo, width, w_ref=w_in_ref):
        return _dot(hn_b, w_ref[:, lo:lo + width])

    g1 = proj(0, LANES, w_g1_ref)
    q_all = proj(OFF_Q, B_KEY_DIM) * (B_HEAD_K ** -0.5)
    k_all = proj(OFF_K, B_KEY_DIM)
    gk = _dot(g1.astype(BF16), wg2_ref[:, 0:B_KEY_DIM]) + bbg_ref[...]
    p_bv = proj(OFF_BV, B_VAL_DIM)
    log2_a = (jnp.minimum(gk, 0.0) - jnp.log1p(jnp.exp(-jnp.abs(gk)))) * (LOG2E / B_GATE_NORMALIZER)
    cc = GLA_CHUNK
    gla_chunks = tm // cc
    tril_b = (lax.broadcasted_iota(jnp.int32, (cc, cc), 0)
              >= lax.broadcasted_iota(jnp.int32, (cc, cc), 1)).astype(BF16)
    la_hi, la_lo = _split_bf16(log2_a)
    p_u = proj(OFF_U, D_MODEL)
    b_chunks = [_dot(tril_b, la_hi[ci * cc:(ci + 1) * cc]) + _dot(tril_b, la_lo[ci * cc:(ci + 1) * cc])
                for ci in range(gla_chunks)]
    v_all = p_bv.astype(BF16)
    p_v = proj(OFF_V, D_MODEL)

    head_args = []
    for ci in range(gla_chunks):
        rs = slice(ci * cc, (ci + 1) * cc)
        for h in range(B_HEADS):
            ks = slice(h * B_HEAD_K, (h + 1) * B_HEAD_K)
            head_args.append((q_all[rs, ks], k_all[rs, ks], b_chunks[ci][:, ks],
                              v_all[rs, h * B_HEAD_V:(h + 1) * B_HEAD_V]))

    p_z = proj(OFF_Z, D_MODEL)
    a_u = _gelu(p_u)
    a_v = _gelu(p_v)
    p_ma = proj(OFF_MA, D_MODEL, w_mg_ref)
    a_vc = a_v - jnp.mean(a_v, axis=-1, keepdims=True)
    a_vn = (a_vc * lax.rsqrt(jnp.mean(a_vc * a_vc, axis=-1, keepdims=True) + EPS)
            * alng_ref[...] + alnb_ref[...]).astype(BF16)

    o_inter = [_gla_inter_chunk(qs, k, v_b, b, state_ref, i % B_HEADS)
               for i, (qs, k, b, v_b) in enumerate(head_args)]

    tri = (lax.broadcasted_iota(jnp.int32, (A_CHUNK, A_CHUNK), 0)
           >= lax.broadcasted_iota(jnp.int32, (A_CHUNK, A_CHUNK), 1))
    n_chunks = tm // A_CHUNK
    mixed_g = []
    for g in range(A_GROUPS):
        cols = slice(g * A_GROUP_DIM, (g + 1) * A_GROUP_DIM)
        v_g = jnp.concatenate([a_vn[ci * A_CHUNK:(ci + 1) * A_CHUNK, cols] for ci in range(n_chunks)], axis=1)
        m_g = _dot(jnp.where(tri, aws_ref[g], 0.0).astype(BF16), v_g)
        mixed_g.append(jnp.concatenate(
            [m_g[:, ci * A_GROUP_DIM:(ci + 1) * A_GROUP_DIM] for ci in range(n_chunks)], axis=0) + jnp.concatenate(
            [abs_ref[g]] * n_chunks, axis=0))
    mixed = jnp.concatenate(mixed_g, axis=1)
    y_a = a_u * mixed * _silu(p_z)
    g_a = jax.nn.sigmoid(p_ma + bm_ref[:, 0:D_MODEL])
    merged = g_a * _dot(y_a.astype(BF16), wpa_ref[:, 0:D_MODEL])

    for i, (qs, k, b, _) in enumerate(head_args):
        scores_ref[i] = _scores_single_reference(qs, k, b)
    span = jnp.zeros((1, B_KEY_DIM), F32)
    for b in b_chunks:
        b_mid = b[cc // 2 - 1:cc // 2, :]
        span = jnp.maximum(span, jnp.maximum(-b_mid, b_mid - b[cc - 1:cc, :]))
    wide_span = jnp.max(span) >= FAST_SPAN_LOG2

    @pl.when(wide_span)
    def _():
        for i, (qs, k, b, _) in enumerate(head_args):
            scores_ref[i] = _scores_hierarchical(qs, k, b)

    p_bz = proj(OFF_BZ, B_VAL_DIM)
    heads = []
    for i, (_, _, _, v_b) in enumerate(head_args):
        o_h = o_inter[i] + _dot(scores_ref[i], v_b)
        heads.append(_rms(o_h) * bng_ref[...])
    p_mb = proj(OFF_MB, D_MODEL, w_mg_ref)
    o_attn = jnp.concatenate([jnp.concatenate(heads[ci * B_HEADS:(ci + 1) * B_HEADS], axis=1)
                              for ci in range(gla_chunks)], axis=0)
    o_b = o_attn * _silu(p_bz)
    y_b = _dot(o_b.astype(BF16), wpb_ref[:, 0:D_MODEL])
    g_b = jax.nn.sigmoid(p_mb + bm_ref[:, D_MODEL:2 * D_MODEL])
    merged = merged + g_b * y_b

    out = _dot(merged.astype(BF16), wout_ref[:, 0:D_MODEL])
    o_ref[0] = _rms(x + gate * out) * fg_ref[...]


def _bf16_weight(w):
    tiles = -(-w.shape[1] // LANES)
    tiles += 1 - tiles % 2
    return jnp.pad(w, ((0, 0), (0, tiles * LANES - w.shape[1]))).astype(BF16)


def _const_spec(shape):
    nd = len(shape)
    return pl.BlockSpec(shape, lambda b, j: (0,) * nd, pipeline_mode=pl.Buffered(1))


def kernel(x, c, w_ada, b_ada, norm_g, w_in, a_ln_g, a_ln_b, a_w_s, a_b_s, b_w_gate2, b_b_gate, b_norm_g, w_proj_a, w_proj_b, b_merge, w_out, final_g):
    bsz, seq, d = x.shape
    assert d == D_MODEL and seq % TILE_M == 0 and w_ada.shape[0] == 1
    l = 0

    c_pad = jnp.zeros((SUBLANES, d), F32).at[:bsz].set(c)
    mod_cols = 1024
    mod = pl.pallas_call(
        _mod_kernel,
        grid=(3 * d // mod_cols,),
        in_specs=[pl.BlockSpec((SUBLANES, d), lambda n: (0, 0)),
                  pl.BlockSpec((d, mod_cols), lambda n: (0, n)),
                  pl.BlockSpec((1, mod_cols), lambda n: (0, n))],
        out_specs=pl.BlockSpec((SUBLANES, mod_cols), lambda n: (0, n)),
        out_shape=jax.ShapeDtypeStruct((SUBLANES, 3 * d), F32),
        name="adaln_mod",
    )(c_pad, w_ada[l], b_ada[l][None, :])
    mod = mod[:bsz, None, :]

    wl = w_in[l]
    w_main = _bf16_weight(wl[:, :W_MAIN])
    w_merge = _bf16_weight(wl[:, W_MAIN + B_GATE_RANK:])
    w_g1 = _bf16_weight(wl[:, W_MAIN:W_MAIN + B_GATE_RANK])
    wg2 = _bf16_weight(jnp.pad(b_w_gate2[l], ((0, LANES - B_GATE_RANK), (0, 0))))
    abs_full = jnp.broadcast_to(a_b_s[l][:, :, None], (A_GROUPS, A_CHUNK, A_GROUP_DIM))

    row2 = lambda v: v.reshape(1, -1)
    operands = (
        x, mod, row2(norm_g[l]), w_main, w_merge, w_g1, row2(a_ln_g[l]), row2(a_ln_b[l]), a_w_s[l], abs_full,
        wg2, row2(b_b_gate[l]), row2(b_norm_g[l]), _bf16_weight(w_proj_a[l]), _bf16_weight(w_proj_b[l]),
        row2(b_merge[l]), _bf16_weight(w_out[l]), row2(final_g),
    )
    in_specs = [
        pl.BlockSpec((1, TILE_M, d), lambda b, j: (b, j, 0)),
        pl.BlockSpec((1, 1, 3 * d), lambda b, j: (b, 0, 0)),
    ] + [_const_spec(op.shape) for op in operands[2:]]

    return pl.pallas_call(
        _block_kernel,
        grid=(bsz, seq // TILE_M),
        in_specs=in_specs,
        out_specs=pl.BlockSpec((1, TILE_M, d), lambda b, j: (b, j, 0)),
        out_shape=jax.ShapeDtypeStruct((bsz, seq, d), x.dtype),
        scratch_shapes=[pltpu.VMEM((B_HEADS, B_HEAD_K, B_HEAD_V), F32),
                        pltpu.VMEM((TILE_M // GLA_CHUNK * B_HEADS, GLA_CHUNK, GLA_CHUNK), BF16)],
        compiler_params=pltpu.CompilerParams(
            dimension_semantics=("arbitrary", "arbitrary"),
            vmem_limit_bytes=VMEM_LIMIT_BYTES),
        name="hybrid_block",
    )(*operands)
```

```python
import math

import jax
import jax.numpy as jnp
from jax import lax
from jax.experimental import pallas as pl
from jax.experimental.pallas import tpu as pltpu

D_MODEL = 1024
A_GROUPS = 8
A_CHUNK = 128
A_GROUP_DIM = D_MODEL // A_GROUPS
B_HEADS = 4
B_KEY_DIM = D_MODEL // 2
B_VAL_DIM = D_MODEL
B_HEAD_K = B_KEY_DIM // B_HEADS
B_HEAD_V = B_VAL_DIM // B_HEADS
B_GATE_RANK = 16
B_GATE_NORMALIZER = 16.0
EPS = 1e-6

LANES = 128
SUBLANES = 8
GLA_CHUNK = 256
TILE_M = 512
VMEM_LIMIT_BYTES = 60 * 1024 * 1024

OFF_U, OFF_V, OFF_Z = 0, 1024, 2048
OFF_Q, OFF_K = 3072, 3584
OFF_BV, OFF_BZ = 4096, 5120
W_MAIN = 6 * D_MODEL
IN_WIDTH = W_MAIN + B_GATE_RANK + 2 * D_MODEL
OFF_G1 = W_MAIN
OFF_MA, OFF_MB = W_MAIN + LANES, W_MAIN + LANES + D_MODEL
W_IN_RESIDENT = OFF_MB + D_MODEL
W_SQ_RESIDENT = D_MODEL + LANES
STAGE_ROWS = 64

F32 = jnp.float32
BF16 = jnp.bfloat16
LOG2E = 1.4426950408889634
FAST_SPAN_LOG2 = 80.0


def _dot(a, b):
    return jnp.dot(a, b, preferred_element_type=F32)


def _dot_nt(a, b):
    return lax.dot_general(a, b, (((1,), (1,)), ((), ())), preferred_element_type=F32)


def _dot_tn(a, b):
    return lax.dot_general(a, b, (((0,), (0,)), ((), ())), preferred_element_type=F32)


def _split_bf16(a):
    hi = a.astype(BF16)
    lo = (a - hi.astype(F32)).astype(BF16)
    return hi, lo


def _gelu(a):
    return 0.5 * a * (1.0 + lax.erf(a * (1.0 / math.sqrt(2.0))))


def _silu(a):
    return a * jax.nn.sigmoid(a)


def _rms(a):
    return a * lax.rsqrt(jnp.mean(a * a, axis=-1, keepdims=True) + EPS)


def _bcast_row_in_blocks(a, blk, row):
    m, n = a.shape
    a3 = a.reshape(m // blk, blk, n)
    return jnp.broadcast_to(a3[:, row:row + 1, :], (m // blk, blk, n)).reshape(m, n)


def _mod_kernel(c_ref, w_ref, b_ref, o_ref):
    c = c_ref[...]
    o_ref[...] = jnp.dot(_silu(c), w_ref[...], preferred_element_type=F32,
                         precision=lax.Precision.HIGHEST) + b_ref[...]


def _scores_single_reference(qs, k, b):
    c = qs.shape[0]
    ref_b = b[c // 2 - 1:c // 2, :]
    s = _dot_nt((qs * jnp.exp2(b - ref_b)).astype(BF16), (k * jnp.exp2(ref_b - b)).astype(BF16))
    causal = (lax.broadcasted_iota(jnp.int32, (c, c), 0) >= lax.broadcasted_iota(jnp.int32, (c, c), 1))
    return jnp.where(causal, s, 0.0).astype(BF16)


def _scores_hierarchical(qs, k, b):
    c = qs.shape[0]
    row = lax.broadcasted_iota(jnp.int32, (c, 1), 0)
    col = lax.broadcasted_iota(jnp.int32, (1, c), 1)
    scores = jnp.zeros((c, c), F32)
    m = SUBLANES
    while m < c:
        ref_b = _bcast_row_in_blocks(b, 2 * m, m - 1)
        e = jnp.exp2(-jnp.abs(b - ref_b))
        siblings = ((row // m) == (col // m) + 1) & ((col // m) % 2 == 0)
        scores = jnp.where(siblings, _dot_nt((qs * e).astype(BF16), (k * e).astype(BF16)), scores)
        m *= 2
    row_in = row % SUBLANES
    lane = lax.broadcasted_iota(jnp.int32, (1, LANES), 1)
    diag = jnp.zeros((c, LANES), F32)
    for j in range(SUBLANES):
        b_j = _bcast_row_in_blocks(b, SUBLANES, j)
        k_j = _bcast_row_in_blocks(k, SUBLANES, j)
        s_j = jnp.sum(qs * k_j * jnp.exp2(jnp.minimum(b - b_j, 0.0)), axis=-1, keepdims=True)
        diag = jnp.where((lane == j) & (row_in >= j), s_j, diag)
    expand = (lax.broadcasted_iota(jnp.int32, (LANES, c), 0)
              == lax.broadcasted_iota(jnp.int32, (LANES, c), 1) % SUBLANES).astype(BF16)
    diag_full = _dot(diag.astype(BF16), expand)
    same_block = (row // SUBLANES) == (col // SUBLANES)
    return jnp.where(same_block, diag_full.astype(BF16), scores.astype(BF16))


def _gla_inter_chunk(qs, k, v_b, b, state_ref, h):
    c = qs.shape[0]
    state = state_ref[h]
    o = _dot((qs * jnp.exp2(b)).astype(BF16), state.astype(BF16))
    b_last = b[c - 1:c, :]
    k_dec = (k * jnp.exp2(b_last - b)).astype(BF16)
    decay_col = jnp.transpose(jnp.broadcast_to(jnp.exp2(b_last), (LANES, B_HEAD_K)))[:, 0:1]
    state_ref[h] = decay_col * state + _dot_tn(k_dec, v_b)
    return o


def _stream_rows(src_hbm, width, stage_ref, sem_ref, finish):
    n = D_MODEL // STAGE_ROWS

    def copy(i, slot):
        row0 = pl.multiple_of(i * STAGE_ROWS, STAGE_ROWS)
        return pltpu.make_async_copy(src_hbm.at[0, pl.ds(row0, STAGE_ROWS), :],
                                     stage_ref.at[slot, :, pl.ds(0, width)], sem_ref.at[slot])

    copy(0, 0).start()

    def body(i, carry):
        slot = i % 2

        @pl.when(i + 1 < n)
        def _():
            copy(i + 1, 1 - slot).start()

        copy(i, slot).wait()
        finish(pl.multiple_of(i * STAGE_ROWS, STAGE_ROWS), slot)
        return carry

    lax.fori_loop(0, n, body, 0)


def _stage_weights(w_in_hbm, sq_hbm, w_in_s, w_sq_s, stage_ref, sem_ref):
    def finish_in(row0, slot):
        rows = pl.ds(row0, STAGE_ROWS)
        w_in_s[rows, 0:W_MAIN] = stage_ref[slot, :, 0:W_MAIN].astype(BF16)
        g1 = stage_ref[slot, :, W_MAIN:W_MAIN + LANES]
        lane = lax.broadcasted_iota(jnp.int32, g1.shape, 1)
        w_in_s[rows, OFF_G1:OFF_G1 + LANES] = jnp.where(lane < B_GATE_RANK, g1, 0.0).astype(BF16)
        w_in_s[rows, OFF_MA:W_IN_RESIDENT] = stage_ref[slot, :, W_MAIN + B_GATE_RANK:IN_WIDTH].astype(BF16)

    _stream_rows(w_in_hbm, IN_WIDTH, stage_ref, sem_ref, finish_in)
    for m, w_hbm in enumerate(sq_hbm):
        def finish_sq(row0, slot, m=m):
            w_sq_s[m, pl.ds(row0, STAGE_ROWS), 0:D_MODEL] = stage_ref[slot, :, 0:D_MODEL].astype(BF16)

        _stream_rows(w_hbm, D_MODEL, stage_ref, sem_ref, finish_sq)


def _block_kernel(x_ref, mod_ref, ng_ref, w_in_hbm, alng_ref, alnb_ref, aws_ref, abs_ref,
                  wg2_ref, bbg_ref, bng_ref, wpa_hbm, wpb_hbm, bm_ref, wout_hbm, fg_ref,
                  o_ref, state_ref, scores_ref, w_in_ref, w_sq_ref, stage_ref, sem_ref):
    @pl.when((pl.program_id(0) == 0) & (pl.program_id(1) == 0))
    def _():
        _stage_weights(w_in_hbm, (wpa_hbm, wpb_hbm, wout_hbm), w_in_ref, w_sq_ref, stage_ref, sem_ref)

    @pl.when(pl.program_id(1) == 0)
    def _():
        state_ref[...] = jnp.zeros_like(state_ref)

    tm = x_ref.shape[1]
    x = x_ref[0]
    mod = mod_ref[0]
    shift = mod[:, 0:D_MODEL]
    scale = mod[:, D_MODEL:2 * D_MODEL]
    gate = mod[:, 2 * D_MODEL:3 * D_MODEL]

    hn_b = (_rms(x) * ng_ref[...] * (1.0 + scale) + shift).astype(BF16)

    def proj(lo, width, w_ref=w_in_ref):
        return _dot(hn_b, w_ref[:, lo:lo + width])

    g1 = proj(OFF_G1, LANES)
    q_all = proj(OFF_Q, B_KEY_DIM) * (B_HEAD_K ** -0.5)
    k_all = proj(OFF_K, B_KEY_DIM)
    gk = _dot(g1.astype(BF16), wg2_ref[:, 0:B_KEY_DIM]) + bbg_ref[...]
    p_bv = proj(OFF_BV, B_VAL_DIM)
    log2_a = (jnp.minimum(gk, 0.0) - jnp.log1p(jnp.exp(-jnp.abs(gk)))) * (LOG2E / B_GATE_NORMALIZER)
    cc = GLA_CHUNK
    gla_chunks = tm // cc
    tril_b = (lax.broadcasted_iota(jnp.int32, (cc, cc), 0)
              >= lax.broadcasted_iota(jnp.int32, (cc, cc), 1)).astype(BF16)
    la_hi, la_lo = _split_bf16(log2_a)
    p_u = proj(OFF_U, D_MODEL)
    b_chunks = [_dot(tril_b, la_hi[ci * cc:(ci + 1) * cc]) + _dot(tril_b, la_lo[ci * cc:(ci + 1) * cc])
                for ci in range(gla_chunks)]
    v_all = p_bv.astype(BF16)
    p_v = proj(OFF_V, D_MODEL)

    head_args = []
    for ci in range(gla_chunks):
        rs = slice(ci * cc, (ci + 1) * cc)
        for h in range(B_HEADS):
            ks = slice(h * B_HEAD_K, (h + 1) * B_HEAD_K)
            head_args.append((q_all[rs, ks], k_all[rs, ks], b_chunks[ci][:, ks],
                              v_all[rs, h * B_HEAD_V:(h + 1) * B_HEAD_V]))

    p_z = proj(OFF_Z, D_MODEL)
    a_u = _gelu(p_u)
    a_v = _gelu(p_v)
    p_ma = proj(OFF_MA, D_MODEL)
    a_vc = a_v - jnp.mean(a_v, axis=-1, keepdims=True)
    a_vn = (a_vc * lax.rsqrt(jnp.mean(a_vc * a_vc, axis=-1, keepdims=True) + EPS)
            * alng_ref[...] + alnb_ref[...]).astype(BF16)

    o_inter = [_gla_inter_chunk(qs, k, v_b, b, state_ref, i % B_HEADS)
               for i, (qs, k, b, v_b) in enumerate(head_args)]

    tri = (lax.broadcasted_iota(jnp.int32, (A_CHUNK, A_CHUNK), 0)
           >= lax.broadcasted_iota(jnp.int32, (A_CHUNK, A_CHUNK), 1))
    n_chunks = tm // A_CHUNK
    mixed_g = []
    for g in range(A_GROUPS):
        cols = slice(g * A_GROUP_DIM, (g + 1) * A_GROUP_DIM)
        v_g = jnp.concatenate([a_vn[ci * A_CHUNK:(ci + 1) * A_CHUNK, cols] for ci in range(n_chunks)], axis=1)
        m_g = _dot(jnp.where(tri, aws_ref[g], 0.0).astype(BF16), v_g)
        mixed_g.append(jnp.concatenate(
            [m_g[:, ci * A_GROUP_DIM:(ci + 1) * A_GROUP_DIM] for ci in range(n_chunks)], axis=0) + jnp.concatenate(
            [abs_ref[g]] * n_chunks, axis=0))
    mixed = jnp.concatenate(mixed_g, axis=1)
    y_a = a_u * mixed * _silu(p_z)
    g_a = jax.nn.sigmoid(p_ma + bm_ref[:, 0:D_MODEL])
    merged = g_a * _dot(y_a.astype(BF16), w_sq_ref[0, :, 0:D_MODEL])

    for i, (qs, k, b, _) in enumerate(head_args):
        scores_ref[i] = _scores_single_reference(qs, k, b)
    span = jnp.zeros((1, B_KEY_DIM), F32)
    for b in b_chunks:
        b_mid = b[cc // 2 - 1:cc // 2, :]
        span = jnp.maximum(span, jnp.maximum(-b_mid, b_mid - b[cc - 1:cc, :]))
    wide_span = jnp.max(span) >= FAST_SPAN_LOG2

    @pl.when(wide_span)
    def _():
        for i, (qs, k, b, _) in enumerate(head_args):
            scores_ref[i] = _scores_hierarchical(qs, k, b)

    p_bz = proj(OFF_BZ, B_VAL_DIM)
    heads = []
    for i, (_, _, _, v_b) in enumerate(head_args):
        o_h = o_inter[i] + _dot(scores_ref[i], v_b)
        heads.append(_rms(o_h) * bng_ref[...])
    p_mb = proj(OFF_MB, D_MODEL)
    o_attn = jnp.concatenate([jnp.concatenate(heads[ci * B_HEADS:(ci + 1) * B_HEADS], axis=1)
                              for ci in range(gla_chunks)], axis=0)
    o_b = o_attn * _silu(p_bz)
    y_b = _dot(o_b.astype(BF16), w_sq_ref[1, :, 0:D_MODEL])
    g_b = jax.nn.sigmoid(p_mb + bm_ref[:, D_MODEL:2 * D_MODEL])
    merged = merged + g_b * y_b

    out = _dot(merged.astype(BF16), w_sq_ref[2, :, 0:D_MODEL])
    o_ref[0] = _rms(x + gate * out) * fg_ref[...]


def _bf16_weight(w):
    tiles = -(-w.shape[1] // LANES)
    tiles += 1 - tiles % 2
    return jnp.pad(w, ((0, 0), (0, tiles * LANES - w.shape[1]))).astype(BF16)


def _const_spec(shape):
    nd = len(shape)
    return pl.BlockSpec(shape, lambda b, j: (0,) * nd, pipeline_mode=pl.Buffered(1))


def kernel(x, c, w_ada, b_ada, norm_g, w_in, a_ln_g, a_ln_b, a_w_s, a_b_s, b_w_gate2, b_b_gate, b_norm_g, w_proj_a, w_proj_b, b_merge, w_out, final_g):
    bsz, seq, d = x.shape
    assert d == D_MODEL and seq % TILE_M == 0 and w_ada.shape[0] == 1
    l = 0

    c_pad = jnp.zeros((SUBLANES, d), F32).at[:bsz].set(c)
    mod_cols = 1024
    mod = pl.pallas_call(
        _mod_kernel,
        grid=(3 * d // mod_cols,),
        in_specs=[pl.BlockSpec((SUBLANES, d), lambda n: (0, 0)),
                  pl.BlockSpec((d, mod_cols), lambda n: (0, n)),
                  pl.BlockSpec((1, mod_cols), lambda n: (0, n))],
        out_specs=pl.BlockSpec((SUBLANES, mod_cols), lambda n: (0, n)),
        out_shape=jax.ShapeDtypeStruct((SUBLANES, 3 * d), F32),
        name="adaln_mod",
    )(c_pad, w_ada[l], b_ada[l][None, :])
    mod = mod[:bsz, None, :]

    wg2 = _bf16_weight(jnp.pad(b_w_gate2[l], ((0, LANES - B_GATE_RANK), (0, 0))))
    abs_full = jnp.broadcast_to(a_b_s[l][:, :, None], (A_GROUPS, A_CHUNK, A_GROUP_DIM))

    row2 = lambda v: v.reshape(1, -1)
    in_hbm = {3, 11, 12, 14}
    operands = (
        x, mod, row2(norm_g[l]), w_in, row2(a_ln_g[l]), row2(a_ln_b[l]), a_w_s[l], abs_full,
        wg2, row2(b_b_gate[l]), row2(b_norm_g[l]), w_proj_a, w_proj_b,
        row2(b_merge[l]), w_out, row2(final_g),
    )
    in_specs = [
        pl.BlockSpec((1, TILE_M, d), lambda b, j: (b, j, 0)),
        pl.BlockSpec((1, 1, 3 * d), lambda b, j: (b, 0, 0)),
    ] + [pl.BlockSpec(memory_space=pl.ANY) if i in in_hbm else _const_spec(op.shape)
         for i, op in enumerate(operands) if i >= 2]

    return pl.pallas_call(
        _block_kernel,
        grid=(bsz, seq // TILE_M),
        in_specs=in_specs,
        out_specs=pl.BlockSpec((1, TILE_M, d), lambda b, j: (b, j, 0)),
        out_shape=jax.ShapeDtypeStruct((bsz, seq, d), x.dtype),
        scratch_shapes=[pltpu.VMEM((B_HEADS, B_HEAD_K, B_HEAD_V), F32),
                        pltpu.VMEM((TILE_M // GLA_CHUNK * B_HEADS, GLA_CHUNK, GLA_CHUNK), BF16),
                        pltpu.VMEM((d, W_IN_RESIDENT), BF16),
                        pltpu.VMEM((3, d, W_SQ_RESIDENT), BF16),
                        pltpu.VMEM((2, STAGE_ROWS, IN_WIDTH), F32),
                        pltpu.SemaphoreType.DMA((2,))],
        compiler_params=pltpu.CompilerParams(
            dimension_semantics=("arbitrary", "arbitrary"),
            vmem_limit_bytes=VMEM_LIMIT_BYTES),
        name="hybrid_block",
    )(*operands)
```

```python
import math

import jax
import jax.numpy as jnp
from jax import lax
from jax.experimental import pallas as pl
from jax.experimental.pallas import tpu as pltpu

D_MODEL = 1024
A_GROUPS = 8
A_CHUNK = 128
A_GROUP_DIM = D_MODEL // A_GROUPS
B_HEADS = 4
B_KEY_DIM = D_MODEL // 2
B_VAL_DIM = D_MODEL
B_HEAD_K = B_KEY_DIM // B_HEADS
B_HEAD_V = B_VAL_DIM // B_HEADS
B_GATE_RANK = 16
B_GATE_NORMALIZER = 16.0
EPS = 1e-6

LANES = 128
SUBLANES = 8
GLA_CHUNK = 256
TILE_M = 512
VMEM_LIMIT_BYTES = 60 * 1024 * 1024

OFF_U, OFF_V, OFF_Z = 0, 1024, 2048
OFF_Q, OFF_K = 3072, 3584
OFF_BV, OFF_BZ = 4096, 5120
W_MAIN = 6 * D_MODEL
IN_WIDTH = W_MAIN + B_GATE_RANK + 2 * D_MODEL
OFF_G1 = W_MAIN
OFF_MA, OFF_MB = W_MAIN + LANES, W_MAIN + LANES + D_MODEL
W_IN_RESIDENT = OFF_MB + D_MODEL
W_SQ_RESIDENT = D_MODEL + LANES
STAGE_ROWS = 512

F32 = jnp.float32
BF16 = jnp.bfloat16
LOG2E = 1.4426950408889634
FAST_SPAN_LOG2 = 80.0


def _dot(a, b):
    return jnp.dot(a, b, preferred_element_type=F32)


def _dot_nt(a, b):
    return lax.dot_general(a, b, (((1,), (1,)), ((), ())), preferred_element_type=F32)


def _dot_tn(a, b):
    return lax.dot_general(a, b, (((0,), (0,)), ((), ())), preferred_element_type=F32)


def _split_bf16(a):
    hi = a.astype(BF16)
    lo = (a - hi.astype(F32)).astype(BF16)
    return hi, lo


def _gelu(a):
    return 0.5 * a * (1.0 + lax.erf(a * (1.0 / math.sqrt(2.0))))


def _silu(a):
    return a * jax.nn.sigmoid(a)


def _rms(a):
    return a * lax.rsqrt(jnp.mean(a * a, axis=-1, keepdims=True) + EPS)


def _bcast_row_in_blocks(a, blk, row):
    m, n = a.shape
    a3 = a.reshape(m // blk, blk, n)
    return jnp.broadcast_to(a3[:, row:row + 1, :], (m // blk, blk, n)).reshape(m, n)


def _mod_kernel(c_ref, w_ref, b_ref, o_ref):
    c = c_ref[...]
    o_ref[...] = jnp.dot(_silu(c), w_ref[...], preferred_element_type=F32,
                         precision=lax.Precision.HIGHEST) + b_ref[...]


def _scores_single_reference(qs, k, b):
    c = qs.shape[0]
    ref_b = b[c // 2 - 1:c // 2, :]
    s = _dot_nt((qs * jnp.exp2(b - ref_b)).astype(BF16), (k * jnp.exp2(ref_b - b)).astype(BF16))
    causal = (lax.broadcasted_iota(jnp.int32, (c, c), 0) >= lax.broadcasted_iota(jnp.int32, (c, c), 1))
    return jnp.where(causal, s, 0.0).astype(BF16)


def _scores_hierarchical(qs, k, b):
    c = qs.shape[0]
    row = lax.broadcasted_iota(jnp.int32, (c, 1), 0)
    col = lax.broadcasted_iota(jnp.int32, (1, c), 1)
    scores = jnp.zeros((c, c), F32)
    m = SUBLANES
    while m < c:
        ref_b = _bcast_row_in_blocks(b, 2 * m, m - 1)
        e = jnp.exp2(-jnp.abs(b - ref_b))
        siblings = ((row // m) == (col // m) + 1) & ((col // m) % 2 == 0)
        scores = jnp.where(siblings, _dot_nt((qs * e).astype(BF16), (k * e).astype(BF16)), scores)
        m *= 2
    row_in = row % SUBLANES
    lane = lax.broadcasted_iota(jnp.int32, (1, LANES), 1)
    diag = jnp.zeros((c, LANES), F32)
    for j in range(SUBLANES):
        b_j = _bcast_row_in_blocks(b, SUBLANES, j)
        k_j = _bcast_row_in_blocks(k, SUBLANES, j)
        s_j = jnp.sum(qs * k_j * jnp.exp2(jnp.minimum(b - b_j, 0.0)), axis=-1, keepdims=True)
        diag = jnp.where((lane == j) & (row_in >= j), s_j, diag)
    expand = (lax.broadcasted_iota(jnp.int32, (LANES, c), 0)
              == lax.broadcasted_iota(jnp.int32, (LANES, c), 1) % SUBLANES).astype(BF16)
    diag_full = _dot(diag.astype(BF16), expand)
    same_block = (row // SUBLANES) == (col // SUBLANES)
    return jnp.where(same_block, diag_full.astype(BF16), scores.astype(BF16))


def _gla_inter_chunk(qs, k, v_b, b, state_ref, h):
    c = qs.shape[0]
    state = state_ref[h]
    o = _dot((qs * jnp.exp2(b)).astype(BF16), state.astype(BF16))
    b_last = b[c - 1:c, :]
    k_dec = (k * jnp.exp2(b_last - b)).astype(BF16)
    decay_col = jnp.transpose(jnp.broadcast_to(jnp.exp2(b_last), (LANES, B_HEAD_K)))[:, 0:1]
    state_ref[h] = decay_col * state + _dot_tn(k_dec, v_b)
    return o


def _stage_weights(w_in_t_hbm, sq_hbm, w_in_s, w_sq_s, stage_ref, sem_ref):
    tasks = []

    def in_task(src_row0, rows, dst_col0, valid):
        def consume(slot):
            blk = jnp.transpose(stage_ref[slot, 0:rows, :])
            if valid < rows:
                blk = jnp.where(lax.broadcasted_iota(jnp.int32, blk.shape, 1) < valid, blk, 0.0)
            w_in_s[:, dst_col0:dst_col0 + rows] = blk.astype(BF16)
        tasks.append((w_in_t_hbm.at[0, pl.ds(src_row0, rows), :], rows, consume))

    for c0 in range(0, W_MAIN, STAGE_ROWS):
        in_task(c0, STAGE_ROWS, c0, STAGE_ROWS)
    in_task(W_MAIN, LANES, OFF_G1, B_GATE_RANK)
    for c0 in range(0, 2 * D_MODEL, STAGE_ROWS):
        in_task(W_MAIN + B_GATE_RANK + c0, STAGE_ROWS, OFF_MA + c0, STAGE_ROWS)
    for m, w_hbm in enumerate(sq_hbm):
        for r0 in range(0, D_MODEL, STAGE_ROWS):
            def consume(slot, m=m, r0=r0):
                w_sq_s[m, r0:r0 + STAGE_ROWS, 0:D_MODEL] = stage_ref[slot].astype(BF16)
            tasks.append((w_hbm.at[0, pl.ds(r0, STAGE_ROWS), :], STAGE_ROWS, consume))

    copies = [pltpu.make_async_copy(src, stage_ref.at[i % 2, pl.ds(0, rows), :], sem_ref.at[i % 2])
              for i, (src, rows, _) in enumerate(tasks)]
    copies[0].start()
    for i, (_, _, consume) in enumerate(tasks):
        if i + 1 < len(tasks):
            copies[i + 1].start()
        copies[i].wait()
        consume(i % 2)


def _block_kernel(x_ref, mod_ref, ng_ref, w_in_hbm, alng_ref, alnb_ref, aws_ref, abs_ref,
                  wg2_ref, bbg_ref, bng_ref, wpa_hbm, wpb_hbm, bm_ref, wout_hbm, fg_ref,
                  o_ref, state_ref, scores_ref, w_in_ref, w_sq_ref, stage_ref, sem_ref):
    @pl.when((pl.program_id(0) == 0) & (pl.program_id(1) == 0))
    def _():
        _stage_weights(w_in_hbm, (wpa_hbm, wpb_hbm, wout_hbm), w_in_ref, w_sq_ref, stage_ref, sem_ref)

    @pl.when(pl.program_id(1) == 0)
    def _():
        state_ref[...] = jnp.zeros_like(state_ref)

    tm = x_ref.shape[1]
    x = x_ref[0]
    mod = mod_ref[0]
    shift = mod[:, 0:D_MODEL]
    scale = mod[:, D_MODEL:2 * D_MODEL]
    gate = mod[:, 2 * D_MODEL:3 * D_MODEL]

    hn_b = (_rms(x) * ng_ref[...] * (1.0 + scale) + shift).astype(BF16)

    def proj(lo, width, w_ref=w_in_ref):
        return _dot(hn_b, w_ref[:, lo:lo + width])

    g1 = proj(OFF_G1, LANES)
    q_all = proj(OFF_Q, B_KEY_DIM) * (B_HEAD_K ** -0.5)
    k_all = proj(OFF_K, B_KEY_DIM)
    gk = _dot(g1.astype(BF16), wg2_ref[:, 0:B_KEY_DIM]) + bbg_ref[...]
    p_bv = proj(OFF_BV, B_VAL_DIM)
    log2_a = (jnp.minimum(gk, 0.0) - jnp.log1p(jnp.exp(-jnp.abs(gk)))) * (LOG2E / B_GATE_NORMALIZER)
    cc = GLA_CHUNK
    gla_chunks = tm // cc
    tril_b = (lax.broadcasted_iota(jnp.int32, (cc, cc), 0)
              >= lax.broadcasted_iota(jnp.int32, (cc, cc), 1)).astype(BF16)
    la_hi, la_lo = _split_bf16(log2_a)
    p_u = proj(OFF_U, D_MODEL)
    b_chunks = [_dot(tril_b, la_hi[ci * cc:(ci + 1) * cc]) + _dot(tril_b, la_lo[ci * cc:(ci + 1) * cc])
                for ci in range(gla_chunks)]
    v_all = p_bv.astype(BF16)
    p_v = proj(OFF_V, D_MODEL)

    head_args = []
    for ci in range(gla_chunks):
        rs = slice(ci * cc, (ci + 1) * cc)
        for h in range(B_HEADS):
            ks = slice(h * B_HEAD_K, (h + 1) * B_HEAD_K)
            head_args.append((q_all[rs, ks], k_all[rs, ks], b_chunks[ci][:, ks],
                              v_all[rs, h * B_HEAD_V:(h + 1) * B_HEAD_V]))

    p_z = proj(OFF_Z, D_MODEL)
    a_u = _gelu(p_u)
    a_v = _gelu(p_v)
    p_ma = proj(OFF_MA, D_MODEL)
    a_vc = a_v - jnp.mean(a_v, axis=-1, keepdims=True)
    a_vn = (a_vc * lax.rsqrt(jnp.mean(a_vc * a_vc, axis=-1, keepdims=True) + EPS)
            * alng_ref[...] + alnb_ref[...]).astype(BF16)

    o_inter = [_gla_inter_chunk(qs, k, v_b, b, state_ref, i % B_HEADS)
               for i, (qs, k, b, v_b) in enumerate(head_args)]

    tri = (lax.broadcasted_iota(jnp.int32, (A_CHUNK, A_CHUNK), 0)
           >= lax.broadcasted_iota(jnp.int32, (A_CHUNK, A_CHUNK), 1))
    n_chunks = tm // A_CHUNK
    mixed_g = []
    for g in range(A_GROUPS):
        cols = slice(g * A_GROUP_DIM, (g + 1) * A_GROUP_DIM)
        v_g = jnp.concatenate([a_vn[ci * A_CHUNK:(ci + 1) * A_CHUNK, cols] for ci in range(n_chunks)], axis=1)
        m_g = _dot(jnp.where(tri, aws_ref[g], 0.0).astype(BF16), v_g)
        mixed_g.append(jnp.concatenate(
            [m_g[:, ci * A_GROUP_DIM:(ci + 1) * A_GROUP_DIM] for ci in range(n_chunks)], axis=0) + jnp.concatenate(
            [abs_ref[g]] * n_chunks, axis=0))
    mixed = jnp.concatenate(mixed_g, axis=1)
    y_a = a_u * mixed * _silu(p_z)
    g_a = jax.nn.sigmoid(p_ma + bm_ref[:, 0:D_MODEL])
    merged = g_a * _dot(y_a.astype(BF16), w_sq_ref[0, :, 0:D_MODEL])

    for i, (qs, k, b, _) in enumerate(head_args):
        scores_ref[i] = _scores_single_reference(qs, k, b)
    span = jnp.zeros((1, B_KEY_DIM), F32)
    for b in b_chunks:
        b_mid = b[cc // 2 - 1:cc // 2, :]
        span = jnp.maximum(span, jnp.maximum(-b_mid, b_mid - b[cc - 1:cc, :]))
    wide_span = jnp.max(span) >= FAST_SPAN_LOG2

    @pl.when(wide_span)
    def _():
        for i, (qs, k, b, _) in enumerate(head_args):
            scores_ref[i] = _scores_hierarchical(qs, k, b)

    p_bz = proj(OFF_BZ, B_VAL_DIM)
    heads = []
    for i, (_, _, _, v_b) in enumerate(head_args):
        o_h = o_inter[i] + _dot(scores_ref[i], v_b)
        heads.append(_rms(o_h) * bng_ref[...])
    p_mb = proj(OFF_MB, D_MODEL)
    o_attn = jnp.concatenate([jnp.concatenate(heads[ci * B_HEADS:(ci + 1) * B_HEADS], axis=1)
                              for ci in range(gla_chunks)], axis=0)
    o_b = o_attn * _silu(p_bz)
    y_b = _dot(o_b.astype(BF16), w_sq_ref[1, :, 0:D_MODEL])
    g_b = jax.nn.sigmoid(p_mb + bm_ref[:, D_MODEL:2 * D_MODEL])
    merged = merged + g_b * y_b

    out = _dot(merged.astype(BF16), w_sq_ref[2, :, 0:D_MODEL])
    o_ref[0] = _rms(x + gate * out) * fg_ref[...]


def _bf16_weight(w):
    tiles = -(-w.shape[1] // LANES)
    tiles += 1 - tiles % 2
    return jnp.pad(w, ((0, 0), (0, tiles * LANES - w.shape[1]))).astype(BF16)


def _const_spec(shape):
    nd = len(shape)
    return pl.BlockSpec(shape, lambda b, j: (0,) * nd, pipeline_mode=pl.Buffered(1))


def kernel(x, c, w_ada, b_ada, norm_g, w_in, a_ln_g, a_ln_b, a_w_s, a_b_s, b_w_gate2, b_b_gate, b_norm_g, w_proj_a, w_proj_b, b_merge, w_out, final_g):
    bsz, seq, d = x.shape
    assert d == D_MODEL and seq % TILE_M == 0 and w_ada.shape[0] == 1
    l = 0

    c_pad = jnp.zeros((SUBLANES, d), F32).at[:bsz].set(c)
    mod_cols = 1024
    mod = pl.pallas_call(
        _mod_kernel,
        grid=(3 * d // mod_cols,),
        in_specs=[pl.BlockSpec((SUBLANES, d), lambda n: (0, 0)),
                  pl.BlockSpec((d, mod_cols), lambda n: (0, n)),
                  pl.BlockSpec((1, mod_cols), lambda n: (0, n))],
        out_specs=pl.BlockSpec((SUBLANES, mod_cols), lambda n: (0, n)),
        out_shape=jax.ShapeDtypeStruct((SUBLANES, 3 * d), F32),
        name="adaln_mod",
    )(c_pad, w_ada[l], b_ada[l][None, :])
    mod = mod[:bsz, None, :]

    wg2 = _bf16_weight(jnp.pad(b_w_gate2[l], ((0, LANES - B_GATE_RANK), (0, 0))))
    abs_full = jnp.broadcast_to(a_b_s[l][:, :, None], (A_GROUPS, A_CHUNK, A_GROUP_DIM))

    row2 = lambda v: v.reshape(1, -1)
    in_hbm = {3, 11, 12, 14}
    operands = (
        x, mod, row2(norm_g[l]), jnp.swapaxes(w_in, 1, 2), row2(a_ln_g[l]), row2(a_ln_b[l]), a_w_s[l], abs_full,
        wg2, row2(b_b_gate[l]), row2(b_norm_g[l]), w_proj_a, w_proj_b,
        row2(b_merge[l]), w_out, row2(final_g),
    )
    in_specs = [
        pl.BlockSpec((1, TILE_M, d), lambda b, j: (b, j, 0)),
        pl.BlockSpec((1, 1, 3 * d), lambda b, j: (b, 0, 0)),
    ] + [pl.BlockSpec(memory_space=pl.ANY) if i in in_hbm else _const_spec(op.shape)
         for i, op in enumerate(operands) if i >= 2]

    return pl.pallas_call(
        _block_kernel,
        grid=(bsz, seq // TILE_M),
        in_specs=in_specs,
        out_specs=pl.BlockSpec((1, TILE_M, d), lambda b, j: (b, j, 0)),
        out_shape=jax.ShapeDtypeStruct((bsz, seq, d), x.dtype),
        scratch_shapes=[pltpu.VMEM((B_HEADS, B_HEAD_K, B_HEAD_V), F32),
                        pltpu.VMEM((TILE_M // GLA_CHUNK * B_HEADS, GLA_CHUNK, GLA_CHUNK), BF16),
                        pltpu.VMEM((d, W_IN_RESIDENT), BF16),
                        pltpu.VMEM((3, d, W_SQ_RESIDENT), BF16),
                        pltpu.VMEM((2, STAGE_ROWS, D_MODEL), F32),
                        pltpu.SemaphoreType.DMA((2,))],
        compiler_params=pltpu.CompilerParams(
            dimension_semantics=("arbitrary", "arbitrary"),
            vmem_limit_bytes=VMEM_LIMIT_BYTES),
        name="hybrid_block",
    )(*operands)
```

```python
import math

import jax
import jax.numpy as jnp
from jax import lax
from jax.experimental import pallas as pl
from jax.experimental.pallas import tpu as pltpu

D_MODEL = 1024
A_GROUPS = 8
A_CHUNK = 128
A_GROUP_DIM = D_MODEL // A_GROUPS
B_HEADS = 4
B_KEY_DIM = D_MODEL // 2
B_VAL_DIM = D_MODEL
B_HEAD_K = B_KEY_DIM // B_HEADS
B_HEAD_V = B_VAL_DIM // B_HEADS
B_GATE_RANK = 16
B_GATE_NORMALIZER = 16.0
EPS = 1e-6

LANES = 128
SUBLANES = 8
GLA_CHUNK = 256
TILE_M = 512
VMEM_LIMIT_BYTES = 60 * 1024 * 1024

OFF_U, OFF_V, OFF_Z = 0, 1024, 2048
OFF_Q, OFF_K = 3072, 3584
OFF_BV, OFF_BZ = 4096, 5120
W_MAIN = 6 * D_MODEL
IN_WIDTH = W_MAIN + B_GATE_RANK + 2 * D_MODEL
OFF_G1 = W_MAIN
OFF_MA, OFF_MB = W_MAIN + LANES, W_MAIN + LANES + D_MODEL
W_IN_RESIDENT = OFF_MB + D_MODEL
W_SQ_RESIDENT = D_MODEL + LANES
STAGE_ROWS = 512

F32 = jnp.float32
BF16 = jnp.bfloat16
LOG2E = 1.4426950408889634
FAST_SPAN_LOG2 = 80.0


def _dot(a, b):
    return jnp.dot(a, b, preferred_element_type=F32)


def _dot_nt(a, b):
    return lax.dot_general(a, b, (((1,), (1,)), ((), ())), preferred_element_type=F32)


def _dot_tn(a, b):
    return lax.dot_general(a, b, (((0,), (0,)), ((), ())), preferred_element_type=F32)


def _split_bf16(a):
    hi = a.astype(BF16)
    lo = (a - hi.astype(F32)).astype(BF16)
    return hi, lo


def _gelu(a):
    return 0.5 * a * (1.0 + lax.erf(a * (1.0 / math.sqrt(2.0))))


def _silu(a):
    return a * jax.nn.sigmoid(a)


def _rms(a):
    return a * lax.rsqrt(jnp.mean(a * a, axis=-1, keepdims=True) + EPS)


def _bcast_row_in_blocks(a, blk, row):
    m, n = a.shape
    a3 = a.reshape(m // blk, blk, n)
    return jnp.broadcast_to(a3[:, row:row + 1, :], (m // blk, blk, n)).reshape(m, n)


def _mod_kernel(c_ref, w_ref, b_ref, o_ref):
    c = c_ref[...]
    o_ref[...] = jnp.dot(_silu(c), w_ref[...], preferred_element_type=F32,
                         precision=lax.Precision.HIGHEST) + b_ref[...]


def _scores_single_reference(qs, k, b):
    c = qs.shape[0]
    ref_b = b[c // 2 - 1:c // 2, :]
    s = _dot_nt((qs * jnp.exp2(b - ref_b)).astype(BF16), (k * jnp.exp2(ref_b - b)).astype(BF16))
    causal = (lax.broadcasted_iota(jnp.int32, (c, c), 0) >= lax.broadcasted_iota(jnp.int32, (c, c), 1))
    return jnp.where(causal, s, 0.0).astype(BF16)


def _scores_hierarchical(qs, k, b):
    c = qs.shape[0]
    row = lax.broadcasted_iota(jnp.int32, (c, 1), 0)
    col = lax.broadcasted_iota(jnp.int32, (1, c), 1)
    scores = jnp.zeros((c, c), F32)
    m = SUBLANES
    while m < c:
        ref_b = _bcast_row_in_blocks(b, 2 * m, m - 1)
        e = jnp.exp2(-jnp.abs(b - ref_b))
        siblings = ((row // m) == (col // m) + 1) & ((col // m) % 2 == 0)
        scores = jnp.where(siblings, _dot_nt((qs * e).astype(BF16), (k * e).astype(BF16)), scores)
        m *= 2
    row_in = row % SUBLANES
    lane = lax.broadcasted_iota(jnp.int32, (1, LANES), 1)
    diag = jnp.zeros((c, LANES), F32)
    for j in range(SUBLANES):
        b_j = _bcast_row_in_blocks(b, SUBLANES, j)
        k_j = _bcast_row_in_blocks(k, SUBLANES, j)
        s_j = jnp.sum(qs * k_j * jnp.exp2(jnp.minimum(b - b_j, 0.0)), axis=-1, keepdims=True)
        diag = jnp.where((lane == j) & (row_in >= j), s_j, diag)
    expand = (lax.broadcasted_iota(jnp.int32, (LANES, c), 0)
              == lax.broadcasted_iota(jnp.int32, (LANES, c), 1) % SUBLANES).astype(BF16)
    diag_full = _dot(diag.astype(BF16), expand)
    same_block = (row // SUBLANES) == (col // SUBLANES)
    return jnp.where(same_block, diag_full.astype(BF16), scores.astype(BF16))


def _gla_inter_chunk(qs, k, v_b, b, state_ref, h):
    c = qs.shape[0]
    state = state_ref[h]
    o = _dot((qs * jnp.exp2(b)).astype(BF16), state.astype(BF16))
    b_last = b[c - 1:c, :]
    k_dec = (k * jnp.exp2(b_last - b)).astype(BF16)
    decay_col = jnp.transpose(jnp.broadcast_to(jnp.exp2(b_last), (LANES, B_HEAD_K)))[:, 0:1]
    state_ref[h] = decay_col * state + _dot_tn(k_dec, v_b)
    return o


def _stage_weights(w_in_t_hbm, sq_hbm, w_in_s, w_g1t_s, w_sq_s, stage_ref, sem_ref):
    tasks = []

    def in_task(src_row0, rows, dst_col0):
        def consume(slot):
            w_in_s[:, dst_col0:dst_col0 + rows] = jnp.transpose(stage_ref[slot, 0:rows, :]).astype(BF16)
        tasks.append((w_in_t_hbm.at[0, pl.ds(src_row0, rows), :], rows, consume))

    for c0 in range(0, W_MAIN, STAGE_ROWS):
        in_task(c0, STAGE_ROWS, c0)

    def consume_gate(slot):
        w_g1t_s[...] = stage_ref[slot, 0:B_GATE_RANK, :].astype(BF16)
    tasks.append((w_in_t_hbm.at[0, pl.ds(W_MAIN, B_GATE_RANK), :], B_GATE_RANK, consume_gate))
    for c0 in range(0, 2 * D_MODEL, STAGE_ROWS):
        in_task(W_MAIN + B_GATE_RANK + c0, STAGE_ROWS, OFF_MA + c0)
    for m, w_hbm in enumerate(sq_hbm):
        for r0 in range(0, D_MODEL, STAGE_ROWS):
            def consume(slot, m=m, r0=r0):
                w_sq_s[m, r0:r0 + STAGE_ROWS, 0:D_MODEL] = stage_ref[slot].astype(BF16)
            tasks.append((w_hbm.at[0, pl.ds(r0, STAGE_ROWS), :], STAGE_ROWS, consume))

    copies = [pltpu.make_async_copy(src, stage_ref.at[i % 2, pl.ds(0, rows), :], sem_ref.at[i % 2])
              for i, (src, rows, _) in enumerate(tasks)]
    copies[0].start()
    for i, (_, _, consume) in enumerate(tasks):
        if i + 1 < len(tasks):
            copies[i + 1].start()
        copies[i].wait()
        consume(i % 2)


def _block_kernel(x_ref, mod_ref, ng_ref, w_in_hbm, alng_ref, alnb_ref, aws_ref, abs_ref,
                  wg2_ref, bbg_ref, bng_ref, wpa_hbm, wpb_hbm, bm_ref, wout_hbm, fg_ref,
                  o_ref, state_ref, scores_ref, w_in_ref, w_g1t_ref, w_sq_ref, stage_ref, sem_ref):
    @pl.when((pl.program_id(0) == 0) & (pl.program_id(1) == 0))
    def _():
        _stage_weights(w_in_hbm, (wpa_hbm, wpb_hbm, wout_hbm), w_in_ref, w_g1t_ref, w_sq_ref, stage_ref, sem_ref)

    @pl.when(pl.program_id(1) == 0)
    def _():
        state_ref[...] = jnp.zeros_like(state_ref)

    tm = x_ref.shape[1]
    x = x_ref[0]
    mod = mod_ref[0]
    shift = mod[:, 0:D_MODEL]
    scale = mod[:, D_MODEL:2 * D_MODEL]
    gate = mod[:, 2 * D_MODEL:3 * D_MODEL]

    hn_b = (_rms(x) * (ng_ref[...] * (1.0 + scale)) + shift).astype(BF16)

    def proj(lo, width, w_ref=w_in_ref):
        return _dot(hn_b, w_ref[:, lo:lo + width])

    q_all = proj(OFF_Q, B_KEY_DIM) * (B_HEAD_K ** -0.5)
    k_all = proj(OFF_K, B_KEY_DIM)
    g1_t = _dot_nt(w_g1t_ref[...], hn_b)
    p_bv = proj(OFF_BV, B_VAL_DIM)
    gk = _dot_tn(g1_t.astype(BF16), wg2_ref[:, 0:B_KEY_DIM]) + bbg_ref[...]
    log2_a = (jnp.minimum(gk, 0.0) - jnp.log1p(jnp.exp(-jnp.abs(gk)))) * (LOG2E / B_GATE_NORMALIZER)
    cc = GLA_CHUNK
    gla_chunks = tm // cc
    tril_b = (lax.broadcasted_iota(jnp.int32, (cc, cc), 0)
              >= lax.broadcasted_iota(jnp.int32, (cc, cc), 1)).astype(BF16)
    la_hi, la_lo = _split_bf16(log2_a)
    p_u = proj(OFF_U, D_MODEL)
    b_chunks = [_dot(tril_b, la_hi[ci * cc:(ci + 1) * cc]) + _dot(tril_b, la_lo[ci * cc:(ci + 1) * cc])
                for ci in range(gla_chunks)]
    v_all = p_bv.astype(BF16)
    p_v = proj(OFF_V, D_MODEL)

    head_args = []
    for ci in range(gla_chunks):
        rs = slice(ci * cc, (ci + 1) * cc)
        for h in range(B_HEADS):
            ks = slice(h * B_HEAD_K, (h + 1) * B_HEAD_K)
            head_args.append((q_all[rs, ks], k_all[rs, ks], b_chunks[ci][:, ks],
                              v_all[rs, h * B_HEAD_V:(h + 1) * B_HEAD_V]))

    p_z = proj(OFF_Z, D_MODEL)
    a_u = _gelu(p_u)
    a_v = _gelu(p_v)
    p_ma = proj(OFF_MA, D_MODEL)
    a_vc = a_v - jnp.mean(a_v, axis=-1, keepdims=True)
    a_vn = (a_vc * lax.rsqrt(jnp.mean(a_vc * a_vc, axis=-1, keepdims=True) + EPS)
            * alng_ref[...] + alnb_ref[...]).astype(BF16)

    o_inter = [_gla_inter_chunk(qs, k, v_b, b, state_ref, i % B_HEADS)
               for i, (qs, k, b, v_b) in enumerate(head_args)]

    tri = (lax.broadcasted_iota(jnp.int32, (A_CHUNK, A_CHUNK), 0)
           >= lax.broadcasted_iota(jnp.int32, (A_CHUNK, A_CHUNK), 1))
    n_chunks = tm // A_CHUNK
    mixed_g = []
    for g in range(A_GROUPS):
        cols = slice(g * A_GROUP_DIM, (g + 1) * A_GROUP_DIM)
        v_g = jnp.concatenate([a_vn[ci * A_CHUNK:(ci + 1) * A_CHUNK, cols] for ci in range(n_chunks)], axis=1)
        m_g = _dot(jnp.where(tri, aws_ref[g], 0.0).astype(BF16), v_g)
        mixed_g.append(jnp.concatenate(
            [m_g[:, ci * A_GROUP_DIM:(ci + 1) * A_GROUP_DIM] for ci in range(n_chunks)], axis=0) + jnp.concatenate(
            [abs_ref[g]] * n_chunks, axis=0))
    mixed = jnp.concatenate(mixed_g, axis=1)
    y_a = a_u * mixed * _silu(p_z)
    g_a = jax.nn.sigmoid(p_ma + bm_ref[:, 0:D_MODEL])
    merged = g_a * _dot(y_a.astype(BF16), w_sq_ref[0, :, 0:D_MODEL])

    for i, (qs, k, b, _) in enumerate(head_args):
        scores_ref[i] = _scores_single_reference(qs, k, b)
    span = jnp.zeros((1, B_KEY_DIM), F32)
    for b in b_chunks:
        b_mid = b[cc // 2 - 1:cc // 2, :]
        span = jnp.maximum(span, jnp.maximum(-b_mid, b_mid - b[cc - 1:cc, :]))
    wide_span = jnp.max(span) >= FAST_SPAN_LOG2

    @pl.when(wide_span)
    def _():
        for i, (qs, k, b, _) in enumerate(head_args):
            scores_ref[i] = _scores_hierarchical(qs, k, b)

    p_bz = proj(OFF_BZ, B_VAL_DIM)
    heads = []
    for i, (_, _, _, v_b) in enumerate(head_args):
        o_h = o_inter[i] + _dot(scores_ref[i], v_b)
        heads.append(_rms(o_h) * bng_ref[...])
    p_mb = proj(OFF_MB, D_MODEL)
    o_attn = jnp.concatenate([jnp.concatenate(heads[ci * B_HEADS:(ci + 1) * B_HEADS], axis=1)
                              for ci in range(gla_chunks)], axis=0)
    o_b = o_attn * _silu(p_bz)
    y_b = _dot(o_b.astype(BF16), w_sq_ref[1, :, 0:D_MODEL])
    g_b = jax.nn.sigmoid(p_mb + bm_ref[:, D_MODEL:2 * D_MODEL])
    merged = merged + g_b * y_b

    out = _dot(merged.astype(BF16), w_sq_ref[2, :, 0:D_MODEL])
    o_ref[0] = _rms(x + gate * out) * fg_ref[...]


def _bf16_weight(w):
    tiles = -(-w.shape[1] // LANES)
    tiles += 1 - tiles % 2
    return jnp.pad(w, ((0, 0), (0, tiles * LANES - w.shape[1]))).astype(BF16)


def _const_spec(shape):
    nd = len(shape)
    return pl.BlockSpec(shape, lambda b, j: (0,) * nd, pipeline_mode=pl.Buffered(1))


def kernel(x, c, w_ada, b_ada, norm_g, w_in, a_ln_g, a_ln_b, a_w_s, a_b_s, b_w_gate2, b_b_gate, b_norm_g, w_proj_a, w_proj_b, b_merge, w_out, final_g):
    bsz, seq, d = x.shape
    assert d == D_MODEL and seq % TILE_M == 0 and w_ada.shape[0] == 1
    l = 0

    c_pad = jnp.zeros((SUBLANES, d), F32).at[:bsz].set(c)
    mod_cols = 1024
    mod = pl.pallas_call(
        _mod_kernel,
        grid=(3 * d // mod_cols,),
        in_specs=[pl.BlockSpec((SUBLANES, d), lambda n: (0, 0)),
                  pl.BlockSpec((d, mod_cols), lambda n: (0, n)),
                  pl.BlockSpec((1, mod_cols), lambda n: (0, n))],
        out_specs=pl.BlockSpec((SUBLANES, mod_cols), lambda n: (0, n)),
        out_shape=jax.ShapeDtypeStruct((SUBLANES, 3 * d), F32),
        name="adaln_mod",
    )(c_pad, w_ada[l], b_ada[l][None, :])
    mod = mod[:bsz, None, :]

    wg2 = _bf16_weight(b_w_gate2[l])
    abs_full = jnp.broadcast_to(a_b_s[l][:, :, None], (A_GROUPS, A_CHUNK, A_GROUP_DIM))

    row2 = lambda v: v.reshape(1, -1)
    in_hbm = {3, 11, 12, 14}
    operands = (
        x, mod, row2(norm_g[l]), jnp.swapaxes(w_in, 1, 2), row2(a_ln_g[l]), row2(a_ln_b[l]), a_w_s[l], abs_full,
        wg2, row2(b_b_gate[l]), row2(b_norm_g[l]), w_proj_a, w_proj_b,
        row2(b_merge[l]), w_out, row2(final_g),
    )
    in_specs = [
        pl.BlockSpec((1, TILE_M, d), lambda b, j: (b, j, 0)),
        pl.BlockSpec((1, 1, 3 * d), lambda b, j: (b, 0, 0)),
    ] + [pl.BlockSpec(memory_space=pl.ANY) if i in in_hbm else _const_spec(op.shape)
         for i, op in enumerate(operands) if i >= 2]

    return pl.pallas_call(
        _block_kernel,
        grid=(bsz, seq // TILE_M),
        in_specs=in_specs,
        out_specs=pl.BlockSpec((1, TILE_M, d), lambda b, j: (b, j, 0)),
        out_shape=jax.ShapeDtypeStruct((bsz, seq, d), x.dtype),
        scratch_shapes=[pltpu.VMEM((B_HEADS, B_HEAD_K, B_HEAD_V), F32),
                        pltpu.VMEM((TILE_M // GLA_CHUNK * B_HEADS, GLA_CHUNK, GLA_CHUNK), BF16),
                        pltpu.VMEM((d, W_IN_RESIDENT), BF16),
                        pltpu.VMEM((B_GATE_RANK, d), BF16),
                        pltpu.VMEM((3, d, W_SQ_RESIDENT), BF16),
                        pltpu.VMEM((2, STAGE_ROWS, D_MODEL), F32),
                        pltpu.SemaphoreType.DMA((2,))],
        compiler_params=pltpu.CompilerParams(
            dimension_semantics=("arbitrary", "arbitrary"),
            vmem_limit_bytes=VMEM_LIMIT_BYTES),
        name="hybrid_block",
    )(*operands)
```

```python
import functools
import math

import jax
import jax.numpy as jnp
from jax import lax
from jax.experimental import pallas as pl
from jax.experimental.pallas import tpu as pltpu

D_MODEL = 1024
A_GROUPS = 8
A_CHUNK = 128
A_GROUP_DIM = D_MODEL // A_GROUPS
B_HEADS = 4
B_KEY_DIM = D_MODEL // 2
B_VAL_DIM = D_MODEL
B_HEAD_K = B_KEY_DIM // B_HEADS
B_HEAD_V = B_VAL_DIM // B_HEADS
B_GATE_RANK = 16
B_GATE_NORMALIZER = 16.0
EPS = 1e-6

LANES = 128
SUBLANES = 8
GLA_CHUNK = 256
TILE_M = 512
VMEM_LIMIT_BYTES = 60 * 1024 * 1024

OFF_U, OFF_V, OFF_Z = 0, 1024, 2048
OFF_Q, OFF_K = 3072, 3584
OFF_BV, OFF_BZ = 4096, 5120
W_MAIN = 6 * D_MODEL
IN_WIDTH = W_MAIN + B_GATE_RANK + 2 * D_MODEL
OFF_G1 = W_MAIN
OFF_MA, OFF_MB = W_MAIN + LANES, W_MAIN + LANES + D_MODEL
W_IN_RESIDENT = OFF_MB + D_MODEL
W_SQ_RESIDENT = D_MODEL + LANES
STAGE_ROWS = TILE_M

F32 = jnp.float32
BF16 = jnp.bfloat16
LOG2E = 1.4426950408889634
FAST_SPAN_LOG2 = 80.0


def _dot(a, b):
    return jnp.dot(a, b, preferred_element_type=F32)


def _dot_nt(a, b):
    return lax.dot_general(a, b, (((1,), (1,)), ((), ())), preferred_element_type=F32)


def _dot_tn(a, b):
    return lax.dot_general(a, b, (((0,), (0,)), ((), ())), preferred_element_type=F32)


def _split_bf16(a):
    hi = a.astype(BF16)
    lo = (a - hi.astype(F32)).astype(BF16)
    return hi, lo


def _gelu(a):
    return 0.5 * a * (1.0 + lax.erf(a * (1.0 / math.sqrt(2.0))))


def _silu(a):
    return a * jax.nn.sigmoid(a)


def _rms(a):
    return a * lax.rsqrt(jnp.mean(a * a, axis=-1, keepdims=True) + EPS)


def _bcast_row_in_blocks(a, blk, row):
    m, n = a.shape
    a3 = a.reshape(m // blk, blk, n)
    return jnp.broadcast_to(a3[:, row:row + 1, :], (m // blk, blk, n)).reshape(m, n)


def _mod_kernel(c_ref, w_ref, b_ref, o_ref):
    c = c_ref[...]
    o_ref[...] = jnp.dot(_silu(c), w_ref[...], preferred_element_type=F32,
                         precision=lax.Precision.HIGHEST) + b_ref[...]


def _scores_single_reference(qs, k, b):
    c = qs.shape[0]
    ref_b = b[c // 2 - 1:c // 2, :]
    s = _dot_nt((qs * jnp.exp2(b - ref_b)).astype(BF16), (k * jnp.exp2(ref_b - b)).astype(BF16))
    causal = (lax.broadcasted_iota(jnp.int32, (c, c), 0) >= lax.broadcasted_iota(jnp.int32, (c, c), 1))
    return jnp.where(causal, s, 0.0).astype(BF16)


def _scores_hierarchical(qs, k, b):
    c = qs.shape[0]
    row = lax.broadcasted_iota(jnp.int32, (c, 1), 0)
    col = lax.broadcasted_iota(jnp.int32, (1, c), 1)
    scores = jnp.zeros((c, c), F32)
    m = SUBLANES
    while m < c:
        ref_b = _bcast_row_in_blocks(b, 2 * m, m - 1)
        e = jnp.exp2(-jnp.abs(b - ref_b))
        siblings = ((row // m) == (col // m) + 1) & ((col // m) % 2 == 0)
        scores = jnp.where(siblings, _dot_nt((qs * e).astype(BF16), (k * e).astype(BF16)), scores)
        m *= 2
    row_in = row % SUBLANES
    lane = lax.broadcasted_iota(jnp.int32, (1, LANES), 1)
    diag = jnp.zeros((c, LANES), F32)
    for j in range(SUBLANES):
        b_j = _bcast_row_in_blocks(b, SUBLANES, j)
        k_j = _bcast_row_in_blocks(k, SUBLANES, j)
        s_j = jnp.sum(qs * k_j * jnp.exp2(jnp.minimum(b - b_j, 0.0)), axis=-1, keepdims=True)
        diag = jnp.where((lane == j) & (row_in >= j), s_j, diag)
    expand = (lax.broadcasted_iota(jnp.int32, (LANES, c), 0)
              == lax.broadcasted_iota(jnp.int32, (LANES, c), 1) % SUBLANES).astype(BF16)
    diag_full = _dot(diag.astype(BF16), expand)
    same_block = (row // SUBLANES) == (col // SUBLANES)
    return jnp.where(same_block, diag_full.astype(BF16), scores.astype(BF16))


def _gla_inter_chunk(qs, k, v_b, b, state_ref, h):
    c = qs.shape[0]
    state = state_ref[h]
    o = _dot((qs * jnp.exp2(b)).astype(BF16), state.astype(BF16))
    b_last = b[c - 1:c, :]
    k_dec = (k * jnp.exp2(b_last - b)).astype(BF16)
    decay_col = jnp.transpose(jnp.broadcast_to(jnp.exp2(b_last), (LANES, B_HEAD_K)))[:, 0:1]
    state_ref[h] = decay_col * state + _dot_tn(k_dec, v_b)
    return o


def _stage_weights(w_in_t_hbm, sq_hbm, w_in_s, w_g1t_s, w_sq_s, stage_ref, sem_ref):
    tasks = []

    def in_task(src_row0, rows, dst_col0):
        def consume(slot):
            w_in_s[:, dst_col0:dst_col0 + rows] = jnp.transpose(stage_ref[slot, 0:rows, :]).astype(BF16)
        tasks.append((w_in_t_hbm.at[0, pl.ds(src_row0, rows), :], rows, consume))

    for c0 in range(0, W_MAIN, STAGE_ROWS):
        in_task(c0, STAGE_ROWS, c0)

    def consume_gate(slot):
        w_g1t_s[...] = stage_ref[slot, 0:B_GATE_RANK, :].astype(BF16)
    tasks.append((w_in_t_hbm.at[0, pl.ds(W_MAIN, B_GATE_RANK), :], B_GATE_RANK, consume_gate))
    for c0 in range(0, 2 * D_MODEL, STAGE_ROWS):
        in_task(W_MAIN + B_GATE_RANK + c0, STAGE_ROWS, OFF_MA + c0)
    for m, w_hbm in enumerate(sq_hbm):
        for r0 in range(0, D_MODEL, STAGE_ROWS):
            def consume(slot, m=m, r0=r0):
                w_sq_s[m, r0:r0 + STAGE_ROWS, 0:D_MODEL] = stage_ref[slot].astype(BF16)
            tasks.append((w_hbm.at[0, pl.ds(r0, STAGE_ROWS), :], STAGE_ROWS, consume))

    copies = [pltpu.make_async_copy(src, stage_ref.at[i % 2, pl.ds(0, rows), :], sem_ref.at[i % 2])
              for i, (src, rows, _) in enumerate(tasks)]
    copies[0].start()
    for i, (_, _, consume) in enumerate(tasks):
        if i + 1 < len(tasks):
            copies[i + 1].start()
        copies[i].wait()
        consume(i % 2)


def _modulated_norm(x, mod, ng_ref):
    shift = mod[:, 0:D_MODEL]
    scale = mod[:, D_MODEL:2 * D_MODEL]
    return (_rms(x) * (ng_ref[...] * (1.0 + scale)) + shift).astype(BF16)


def _block_kernel(x_hbm, mod_ref, mod_next_ref, ng_ref, w_in_hbm, alng_ref, alnb_ref, aws_ref, abs_ref,
                  wg2_ref, bbg_ref, bng_ref, wpa_hbm, wpb_hbm, bm_ref, wout_hbm, fg_ref,
                  o_ref, state_ref, scores_ref, w_in_ref, w_g1t_ref, w_sq_ref, stage_ref, sem_ref,
                  hn_ref, xsem_ref, *, n_seq, n_tiles):
    t = pl.program_id(0)
    tm = stage_ref.shape[1]
    slot = t % 2

    def x_copy(tile, dst_slot):
        tile = jnp.minimum(tile, n_tiles - 1)
        row0 = pl.multiple_of((tile % n_seq) * tm, tm)
        return pltpu.make_async_copy(x_hbm.at[tile // n_seq, pl.ds(row0, tm), :],
                                     stage_ref.at[dst_slot], xsem_ref.at[0])

    @pl.when(t == 0)
    def _():
        _stage_weights(w_in_hbm, (wpa_hbm, wpb_hbm, wout_hbm), w_in_ref, w_g1t_ref, w_sq_ref, stage_ref, sem_ref)
        first = x_copy(0, 0)
        first.start()
        first.wait()
        hn_ref[0] = _modulated_norm(stage_ref[0], mod_ref[0], ng_ref)

    @pl.when(t % n_seq == 0)
    def _():
        state_ref[...] = jnp.zeros_like(state_ref)

    next_x = x_copy(t + 1, 1 - slot)
    next_x.start()

    gate = mod_ref[0][:, 2 * D_MODEL:3 * D_MODEL]

    def proj(lo, width, w_ref=w_in_ref):
        return _dot(hn_ref[slot], w_ref[:, lo:lo + width])

    q_all = proj(OFF_Q, B_KEY_DIM) * (B_HEAD_K ** -0.5)
    k_all = proj(OFF_K, B_KEY_DIM)
    g1_t = _dot_nt(w_g1t_ref[...], hn_ref[slot])
    p_bv = proj(OFF_BV, B_VAL_DIM)
    gk = _dot_tn(g1_t.astype(BF16), wg2_ref[:, 0:B_KEY_DIM]) + bbg_ref[...]
    log2_a = (jnp.minimum(gk, 0.0) - jnp.log1p(jnp.exp(-jnp.abs(gk)))) * (LOG2E / B_GATE_NORMALIZER)
    cc = GLA_CHUNK
    gla_chunks = tm // cc
    tril_b = (lax.broadcasted_iota(jnp.int32, (cc, cc), 0)
              >= lax.broadcasted_iota(jnp.int32, (cc, cc), 1)).astype(BF16)
    la_hi, la_lo = _split_bf16(log2_a)
    p_u = proj(OFF_U, D_MODEL)
    b_chunks = [_dot(tril_b, la_hi[ci * cc:(ci + 1) * cc]) + _dot(tril_b, la_lo[ci * cc:(ci + 1) * cc])
                for ci in range(gla_chunks)]
    v_all = p_bv.astype(BF16)
    p_v = proj(OFF_V, D_MODEL)

    head_args = []
    for ci in range(gla_chunks):
        rs = slice(ci * cc, (ci + 1) * cc)
        for h in range(B_HEADS):
            ks = slice(h * B_HEAD_K, (h + 1) * B_HEAD_K)
            head_args.append((q_all[rs, ks], k_all[rs, ks], b_chunks[ci][:, ks],
                              v_all[rs, h * B_HEAD_V:(h + 1) * B_HEAD_V]))

    p_z = proj(OFF_Z, D_MODEL)
    a_u = _gelu(p_u)
    a_v = _gelu(p_v)
    p_ma = proj(OFF_MA, D_MODEL)
    a_vc = a_v - jnp.mean(a_v, axis=-1, keepdims=True)
    a_vn = (a_vc * lax.rsqrt(jnp.mean(a_vc * a_vc, axis=-1, keepdims=True) + EPS)
            * alng_ref[...] + alnb_ref[...]).astype(BF16)

    o_inter = [_gla_inter_chunk(qs, k, v_b, b, state_ref, i % B_HEADS)
               for i, (qs, k, b, v_b) in enumerate(head_args)]

    tri = (lax.broadcasted_iota(jnp.int32, (A_CHUNK, A_CHUNK), 0)
           >= lax.broadcasted_iota(jnp.int32, (A_CHUNK, A_CHUNK), 1))
    n_chunks = tm // A_CHUNK
    mixed_g = []
    for g in range(A_GROUPS):
        cols = slice(g * A_GROUP_DIM, (g + 1) * A_GROUP_DIM)
        v_g = jnp.concatenate([a_vn[ci * A_CHUNK:(ci + 1) * A_CHUNK, cols] for ci in range(n_chunks)], axis=1)
        m_g = _dot(jnp.where(tri, aws_ref[g], 0.0).astype(BF16), v_g)
        mixed_g.append(jnp.concatenate(
            [m_g[:, ci * A_GROUP_DIM:(ci + 1) * A_GROUP_DIM] for ci in range(n_chunks)], axis=0) + jnp.concatenate(
            [abs_ref[g]] * n_chunks, axis=0))
    mixed = jnp.concatenate(mixed_g, axis=1)
    y_a = a_u * mixed * _silu(p_z)
    g_a = jax.nn.sigmoid(p_ma + bm_ref[:, 0:D_MODEL])
    merged = g_a * _dot(y_a.astype(BF16), w_sq_ref[0, :, 0:D_MODEL])

    for i, (qs, k, b, _) in enumerate(head_args):
        scores_ref[i] = _scores_single_reference(qs, k, b)
    span = jnp.zeros((1, B_KEY_DIM), F32)
    for b in b_chunks:
        b_mid = b[cc // 2 - 1:cc // 2, :]
        span = jnp.maximum(span, jnp.maximum(-b_mid, b_mid - b[cc - 1:cc, :]))
    wide_span = jnp.max(span) >= FAST_SPAN_LOG2

    @pl.when(wide_span)
    def _():
        for i, (qs, k, b, _) in enumerate(head_args):
            scores_ref[i] = _scores_hierarchical(qs, k, b)

    next_x.wait()
    p_bz = proj(OFF_BZ, B_VAL_DIM)
    heads = []
    for i, (_, _, _, v_b) in enumerate(head_args):
        o_h = o_inter[i] + _dot(scores_ref[i], v_b)
        heads.append(_rms(o_h) * bng_ref[...])
    p_mb = proj(OFF_MB, D_MODEL)
    o_attn = jnp.concatenate([jnp.concatenate(heads[ci * B_HEADS:(ci + 1) * B_HEADS], axis=1)
                              for ci in range(gla_chunks)], axis=0)
    o_b = o_attn * _silu(p_bz)
    y_b = _dot(o_b.astype(BF16), w_sq_ref[1, :, 0:D_MODEL])

    g_b =jax.nn.sigmoid(p_mb + bm_ref[:, D_MODEL:2 * D_MODEL])
    merged = merged + g_b * y_b

    out = _dot(merged.astype(BF16), w_sq_ref[2, :, 0:D_MODEL])

    hn_next = _modulated_norm(stage_ref[1 - slot], mod_next_ref[0], ng_ref)
    hn_ref[1 - slot] = hn_next
    words = pltpu.bitcast(hn_next, jnp.uint32)
    any_bits = words[0:SUBLANES]
    for r in range(SUBLANES, words.shape[0], SUBLANES):
        any_bits = any_bits | words[r:r + SUBLANES]
    zero = ((any_bits >> 16) >> 16)[0:1, :].astype(F32)
    o_ref[0] = _rms(stage_ref[slot] + (gate + zero) * out) * fg_ref[...]


def _bf16_weight(w):
    tiles = -(-w.shape[1] // LANES)
    tiles += 1 - tiles % 2
    return jnp.pad(w, ((0, 0), (0, tiles * LANES - w.shape[1]))).astype(BF16)


def _const_spec(shape):
    nd = len(shape)
    return pl.BlockSpec(shape, lambda t: (0,) * nd, pipeline_mode=pl.Buffered(1))


def kernel(x, c, w_ada, b_ada, norm_g, w_in, a_ln_g, a_ln_b, a_w_s, a_b_s, b_w_gate2, b_b_gate, b_norm_g, w_proj_a, w_proj_b, b_merge, w_out, final_g):
    bsz, seq, d = x.shape
    assert d == D_MODEL and seq % TILE_M == 0 and w_ada.shape[0] == 1
    l = 0

    c_pad = jnp.zeros((SUBLANES, d), F32).at[:bsz].set(c)
    mod_cols = 1024
    mod = pl.pallas_call(
        _mod_kernel,
        grid=(3 * d // mod_cols,),
        in_specs=[pl.BlockSpec((SUBLANES, d), lambda n: (0, 0)),
                  pl.BlockSpec((d, mod_cols), lambda n: (0, n)),
                  pl.BlockSpec((1, mod_cols), lambda n: (0, n))],
        out_specs=pl.BlockSpec((SUBLANES, mod_cols), lambda n: (0, n)),
        out_shape=jax.ShapeDtypeStruct((SUBLANES, 3 * d), F32),
        name="adaln_mod",
    )(c_pad, w_ada[l], b_ada[l][None, :])
    mod = mod[:bsz, None, :]

    wg2 = _bf16_weight(b_w_gate2[l])
    abs_full = jnp.broadcast_to(a_b_s[l][:, :, None], (A_GROUPS, A_CHUNK, A_GROUP_DIM))

    row2 = lambda v: v.reshape(1, -1)
    n_seq = seq // TILE_M
    n_tiles = bsz * n_seq
    in_hbm = {0, 4, 12, 13, 15}
    operands = (
        x, mod, mod, row2(norm_g[l]), jnp.swapaxes(w_in, 1, 2), row2(a_ln_g[l]), row2(a_ln_b[l]), a_w_s[l],
        abs_full, wg2, row2(b_b_gate[l]), row2(b_norm_g[l]), w_proj_a, w_proj_b,
        row2(b_merge[l]), w_out, row2(final_g),
    )
    mod_specs = {
        1: pl.BlockSpec((1, 1, 3 * d), lambda t: (t // n_seq, 0, 0)),
        2: pl.BlockSpec((1, 1, 3 * d), lambda t: (jnp.minimum(t + 1, n_tiles - 1) // n_seq, 0, 0)),
    }
    in_specs = [pl.BlockSpec(memory_space=pl.ANY) if i in in_hbm else mod_specs.get(i) or _const_spec(op.shape)
                for i, op in enumerate(operands)]

    return pl.pallas_call(
        functools.partial(_block_kernel, n_seq=n_seq, n_tiles=n_tiles),
        grid=(n_tiles,),
        in_specs=in_specs,
        out_specs=pl.BlockSpec((1, TILE_M, d), lambda t: (t // n_seq, t % n_seq, 0)),
        out_shape=jax.ShapeDtypeStruct((bsz, seq, d), x.dtype),
        scratch_shapes=[pltpu.VMEM((B_HEADS, B_HEAD_K, B_HEAD_V), F32),
                        pltpu.VMEM((TILE_M // GLA_CHUNK * B_HEADS, GLA_CHUNK, GLA_CHUNK), BF16),
                        pltpu.VMEM((d, W_IN_RESIDENT), BF16),
                        pltpu.VMEM((B_GATE_RANK, d), BF16),
                        pltpu.VMEM((3, d, W_SQ_RESIDENT), BF16),
                        pltpu.VMEM((2, TILE_M, D_MODEL), F32),
                        pltpu.SemaphoreType.DMA((2,)),
                        pltpu.VMEM((2, TILE_M, D_MODEL), BF16),
                        pltpu.SemaphoreType.DMA((1,))],
        compiler_params=pltpu.CompilerParams(
            dimension_semantics=("arbitrary",),
            vmem_limit_bytes=VMEM_LIMIT_BYTES),
        name="hybrid_block",
    )(*operands)
```

```python
import math

import jax
import jax.numpy as jnp
from jax import lax
from jax.experimental import pallas as pl
from jax.experimental.pallas import tpu as pltpu

D_MODEL = 1024
A_GROUPS = 8
A_CHUNK = 128
A_GROUP_DIM = D_MODEL // A_GROUPS
B_HEADS = 4
B_KEY_DIM = D_MODEL // 2
B_VAL_DIM = D_MODEL
B_HEAD_K = B_KEY_DIM // B_HEADS
B_HEAD_V = B_VAL_DIM // B_HEADS
B_GATE_RANK = 16
B_GATE_NORMALIZER = 16.0
EPS = 1e-6

LANES = 128
SUBLANES = 8
GLA_CHUNK = 256
TILE_M = 512
VMEM_LIMIT_BYTES = 60 * 1024 * 1024

OFF_U, OFF_V, OFF_Z = 0, 1024, 2048
OFF_Q, OFF_K = 3072, 3584
OFF_BV, OFF_BZ = 4096, 5120
W_MAIN = 6 * D_MODEL
IN_WIDTH = W_MAIN + B_GATE_RANK + 2 * D_MODEL
OFF_G1 = W_MAIN
OFF_MA, OFF_MB = W_MAIN + LANES, W_MAIN + LANES + D_MODEL
W_IN_RESIDENT = OFF_MB + D_MODEL
W_SQ_RESIDENT = D_MODEL + LANES
STAGE_ROWS = 512

F32 = jnp.float32
BF16 = jnp.bfloat16
LOG2E = 1.4426950408889634
FAST_SPAN_LOG2 = 80.0


def _dot(a, b):
    return jnp.dot(a, b, preferred_element_type=F32)


def _dot_nt(a, b):
    return lax.dot_general(a, b, (((1,), (1,)), ((), ())), preferred_element_type=F32)


def _dot_tn(a, b):
    return lax.dot_general(a, b, (((0,), (0,)), ((), ())), preferred_element_type=F32)


def _split_bf16(a):
    hi = a.astype(BF16)
    lo = (a - hi.astype(F32)).astype(BF16)
    return hi, lo


def _gelu(a):
    return 0.5 * a * (1.0 + lax.erf(a * (1.0 / math.sqrt(2.0))))


def _silu(a):
    return a * jax.nn.sigmoid(a)


def _rms(a):
    return a * lax.rsqrt(jnp.mean(a * a, axis=-1, keepdims=True) + EPS)


def _bcast_row_in_blocks(a, blk, row):
    m, n = a.shape
    a3 = a.reshape(m // blk, blk, n)
    return jnp.broadcast_to(a3[:, row:row + 1, :], (m // blk, blk, n)).reshape(m, n)


def _scores_single_reference(qs, k, b):
    c = qs.shape[0]
    ref_b = b[c // 2 - 1:c // 2, :]
    s = _dot_nt((qs * jnp.exp2(b - ref_b)).astype(BF16), (k * jnp.exp2(ref_b - b)).astype(BF16))
    causal = (lax.broadcasted_iota(jnp.int32, (c, c), 0) >= lax.broadcasted_iota(jnp.int32, (c, c), 1))
    return jnp.where(causal, s, 0.0).astype(BF16)


def _scores_hierarchical(qs, k, b):
    c = qs.shape[0]
    row = lax.broadcasted_iota(jnp.int32, (c, 1), 0)
    col = lax.broadcasted_iota(jnp.int32, (1, c), 1)
    scores = jnp.zeros((c, c), F32)
    m = SUBLANES
    while m < c:
        ref_b = _bcast_row_in_blocks(b, 2 * m, m - 1)
        e = jnp.exp2(-jnp.abs(b - ref_b))
        siblings = ((row // m) == (col // m) + 1) & ((col // m) % 2 == 0)
        scores = jnp.where(siblings, _dot_nt((qs * e).astype(BF16), (k * e).astype(BF16)), scores)
        m *= 2
    row_in = row % SUBLANES
    lane = lax.broadcasted_iota(jnp.int32, (1, LANES), 1)
    diag = jnp.zeros((c, LANES), F32)
    for j in range(SUBLANES):
        b_j = _bcast_row_in_blocks(b, SUBLANES, j)
        k_j = _bcast_row_in_blocks(k, SUBLANES, j)
        s_j = jnp.sum(qs * k_j * jnp.exp2(jnp.minimum(b - b_j, 0.0)), axis=-1, keepdims=True)
        diag = jnp.where((lane == j) & (row_in >= j), s_j, diag)
    expand = (lax.broadcasted_iota(jnp.int32, (LANES, c), 0)
              == lax.broadcasted_iota(jnp.int32, (LANES, c), 1) % SUBLANES).astype(BF16)
    diag_full = _dot(diag.astype(BF16), expand)
    same_block = (row // SUBLANES) == (col // SUBLANES)
    return jnp.where(same_block, diag_full.astype(BF16), scores.astype(BF16))


def _gla_inter_chunk(qs, k, v_b, b, state_ref, h):
    c = qs.shape[0]
    state = state_ref[h]
    o = _dot((qs * jnp.exp2(b)).astype(BF16), state.astype(BF16))
    b_last = b[c - 1:c, :]
    k_dec = (k * jnp.exp2(b_last - b)).astype(BF16)
    decay_col = jnp.transpose(jnp.broadcast_to(jnp.exp2(b_last), (LANES, B_HEAD_K)))[:, 0:1]
    state_ref[h] = decay_col * state + _dot_tn(k_dec, v_b)
    return o


def _stage_weights(w_in_t_hbm, sq_hbm, w_ada_hbm, c_ref, b_ada_ref, w_in_s, w_sq_s, mod_s, stage_ref, sem_ref):
    tasks = []

    bsz = c_ref.shape[0]
    act = _silu(c_ref[...])
    act = jnp.concatenate([act, jnp.zeros((SUBLANES - bsz, D_MODEL), F32)], axis=0)
    mod_s[...] = jnp.broadcast_to(b_ada_ref[...], mod_s.shape)
    for c0 in range(0, 3 * D_MODEL, D_MODEL):
        for r0 in range(0, D_MODEL, STAGE_ROWS):
            def consume(slot, c0=c0, r0=r0):
                a_hi, a_lo = _split_bf16(act[:, r0:r0 + STAGE_ROWS])
                w_hi, w_lo = _split_bf16(stage_ref[slot])
                mod_s[:, c0:c0 + D_MODEL] += _dot(a_hi, w_hi) + (_dot(a_lo, w_hi) + _dot(a_hi, w_lo))
            tasks.append((w_ada_hbm.at[0, pl.ds(r0, STAGE_ROWS), pl.ds(c0, D_MODEL)], STAGE_ROWS, consume))

    def in_task(src_row0, rows, dst_col0, valid):
        def consume(slot):
            blk = jnp.transpose(stage_ref[slot, 0:rows, :])
            if valid < rows:
                blk = jnp.where(lax.broadcasted_iota(jnp.int32, blk.shape, 1) < valid, blk, 0.0)
            w_in_s[:, dst_col0:dst_col0 + rows] = blk.astype(BF16)
        tasks.append((w_in_t_hbm.at[0, pl.ds(src_row0, rows), :], rows, consume))

    for c0 in range(0, W_MAIN, STAGE_ROWS):
        in_task(c0, STAGE_ROWS, c0, STAGE_ROWS)
    in_task(W_MAIN, LANES, OFF_G1, B_GATE_RANK)
    for c0 in range(0, 2 * D_MODEL, STAGE_ROWS):
        in_task(W_MAIN + B_GATE_RANK + c0, STAGE_ROWS, OFF_MA + c0, STAGE_ROWS)
    for m, w_hbm in enumerate(sq_hbm):
        for r0 in range(0, D_MODEL, STAGE_ROWS):
            def consume(slot, m=m, r0=r0):
                w_sq_s[m, r0:r0 + STAGE_ROWS, 0:D_MODEL] = stage_ref[slot].astype(BF16)
            tasks.append((w_hbm.at[0, pl.ds(r0, STAGE_ROWS), :], STAGE_ROWS, consume))

    copies = [pltpu.make_async_copy(src, stage_ref.at[i % 2, pl.ds(0, rows), :], sem_ref.at[i % 2])
              for i, (src, rows, _) in enumerate(tasks)]
    copies[0].start()
    for i, (_, _, consume) in enumerate(tasks):
        if i + 1 < len(tasks):
            copies[i + 1].start()
        copies[i].wait()
        consume(i % 2)


def _block_kernel(x_ref, c_ref, w_ada_hbm, b_ada_ref, ng_ref, w_in_hbm, alng_ref, alnb_ref, aws_ref, abs_row_ref,
                  wg2_ref, bbg_ref, bng_ref, wpa_hbm, wpb_hbm, bm_ref, wout_hbm, fg_ref,
                  o_ref, state_ref, scores_ref, w_in_ref, w_sq_ref, mod_ref, abs_ref, stage_ref, sem_ref):
    @pl.when((pl.program_id(0) == 0) & (pl.program_id(1) == 0))
    def _():
        _stage_weights(w_in_hbm, (wpa_hbm, wpb_hbm, wout_hbm), w_ada_hbm, c_ref, b_ada_ref,
                       w_in_ref, w_sq_ref, mod_ref, stage_ref, sem_ref)
        for g in range(A_GROUPS):
            abs_ref[g] = jnp.transpose(jnp.broadcast_to(abs_row_ref[g:g + 1, :], (A_GROUP_DIM, A_CHUNK)))

    @pl.when(pl.program_id(1) == 0)
    def _():
        state_ref[...] = jnp.zeros_like(state_ref)

    tm = x_ref.shape[1]
    x = x_ref[0]
    mod = mod_ref[pl.ds(pl.program_id(0), 1), :]
    shift = mod[:, 0:D_MODEL]
    scale = mod[:, D_MODEL:2 * D_MODEL]
    gate = mod[:, 2 * D_MODEL:3 * D_MODEL]

    hn_b = (_rms(x) * ng_ref[...] * (1.0 + scale) + shift).astype(BF16)

    def proj(lo, width, w_ref=w_in_ref):
        return _dot(hn_b, w_ref[:, lo:lo + width])

    g1 = proj(OFF_G1, LANES)
    q_all = proj(OFF_Q, B_KEY_DIM) * (B_HEAD_K ** -0.5)
    k_all = proj(OFF_K, B_KEY_DIM)
    wg2 = jnp.concatenate([wg2_ref[...], jnp.zeros((LANES - B_GATE_RANK, B_KEY_DIM), F32)], axis=0)
    gk = _dot(g1.astype(BF16), wg2.astype(BF16)) + bbg_ref[...]
    p_bv = proj(OFF_BV, B_VAL_DIM)
    log2_a = (jnp.minimum(gk, 0.0) - jnp.log1p(jnp.exp(-jnp.abs(gk)))) * (LOG2E / B_GATE_NORMALIZER)
    cc = GLA_CHUNK
    gla_chunks = tm // cc
    tril_b = (lax.broadcasted_iota(jnp.int32, (cc, cc), 0)
              >= lax.broadcasted_iota(jnp.int32, (cc, cc), 1)).astype(BF16)
    la_hi, la_lo = _split_bf16(log2_a)
    p_u = proj(OFF_U, D_MODEL)
    b_chunks = [_dot(tril_b, la_hi[ci * cc:(ci + 1) * cc]) + _dot(tril_b, la_lo[ci * cc:(ci + 1) * cc])
                for ci in range(gla_chunks)]
    v_all = p_bv.astype(BF16)

    head_args = []
    for ci in range(gla_chunks):
        rs = slice(ci * cc, (ci + 1) * cc)
        for h in range(B_HEADS):
            ks = slice(h * B_HEAD_K, (h + 1) * B_HEAD_K)
            head_args.append((q_all[rs, ks], k_all[rs, ks], b_chunks[ci][:, ks],
                              v_all[rs, h * B_HEAD_V:(h + 1) * B_HEAD_V]))

    p_v = proj(OFF_V, D_MODEL)
    o_inter = [_gla_inter_chunk(qs, k, v_b, b, state_ref, i % B_HEADS)
               for i, (qs, k, b, v_b) in enumerate(head_args)]

    p_z = proj(OFF_Z, D_MODEL)
    a_u = _gelu(p_u)
    a_v = _gelu(p_v)
    p_ma = proj(OFF_MA, D_MODEL)
    a_vc = a_v - jnp.mean(a_v, axis=-1, keepdims=True)
    a_vn = (a_vc * lax.rsqrt(jnp.mean(a_vc * a_vc, axis=-1, keepdims=True) + EPS)
            * alng_ref[...] + alnb_ref[...]).astype(BF16)

    tri = (lax.broadcasted_iota(jnp.int32, (A_CHUNK, A_CHUNK), 0)
           >= lax.broadcasted_iota(jnp.int32, (A_CHUNK, A_CHUNK), 1))
    n_chunks = tm // A_CHUNK
    mixed_g = []
    for g in range(A_GROUPS):
        cols = slice(g * A_GROUP_DIM, (g + 1) * A_GROUP_DIM)
        v_g = jnp.concatenate([a_vn[ci * A_CHUNK:(ci + 1) * A_CHUNK, cols] for ci in range(n_chunks)], axis=1)
        m_g = _dot(jnp.where(tri, aws_ref[g], 0.0).astype(BF16), v_g)
        mixed_g.append(jnp.concatenate(
            [m_g[:, ci * A_GROUP_DIM:(ci + 1) * A_GROUP_DIM] for ci in range(n_chunks)], axis=0) + jnp.concatenate(
            [abs_ref[g]] * n_chunks, axis=0))
    mixed = jnp.concatenate(mixed_g, axis=1)
    y_a = a_u * mixed * _silu(p_z)
    g_a = jax.nn.sigmoid(p_ma + bm_ref[:, 0:D_MODEL])
    merged = g_a * _dot(y_a.astype(BF16), w_sq_ref[0, :, 0:D_MODEL])

    for i, (qs, k, b, _) in enumerate(head_args):
        scores_ref[i] = _scores_single_reference(qs, k, b)
    span = jnp.zeros((1, B_KEY_DIM), F32)
    for b in b_chunks:
        b_mid = b[cc // 2 - 1:cc // 2, :]
        span = jnp.maximum(span, jnp.maximum(-b_mid, b_mid - b[cc - 1:cc, :]))
    wide_span = jnp.max(span) >= FAST_SPAN_LOG2

    @pl.when(wide_span)
    def _():
        for i, (qs, k, b, _) in enumerate(head_args):
            scores_ref[i] = _scores_hierarchical(qs, k, b)

    p_bz = proj(OFF_BZ, B_VAL_DIM)
    heads = []
    for i, (_, _, _, v_b) in enumerate(head_args):
        o_h = o_inter[i] + _dot(scores_ref[i], v_b)
        heads.append(_rms(o_h) * bng_ref[...])
    p_mb = proj(OFF_MB, D_MODEL)
    o_attn = jnp.concatenate([jnp.concatenate(heads[ci * B_HEADS:(ci + 1) * B_HEADS], axis=1)
                              for ci in range(gla_chunks)], axis=0)
    o_b = o_attn * _silu(p_bz)
    y_b = _dot(o_b.astype(BF16), w_sq_ref[1, :, 0:D_MODEL])
    g_b = jax.nn.sigmoid(p_mb + bm_ref[:, D_MODEL:2 * D_MODEL])
    merged = merged + g_b * y_b

    out = _dot(merged.astype(BF16), w_sq_ref[2, :, 0:D_MODEL])
    o_ref[0] = _rms(x + gate * out) * fg_ref[...]


def _const_spec(shape):
    nd = len(shape)
    return pl.BlockSpec(shape, lambda b, j: (0,) * nd, pipeline_mode=pl.Buffered(1))


def kernel(x, c, w_ada, b_ada, norm_g, w_in, a_ln_g, a_ln_b, a_w_s, a_b_s, b_w_gate2, b_b_gate, b_norm_g, w_proj_a, w_proj_b, b_merge, w_out, final_g):
    bsz, seq, d = x.shape
    assert d == D_MODEL and seq % TILE_M == 0 and w_ada.shape[0] == 1 and bsz <= SUBLANES
    l = 0

    row2 = lambda v: v.reshape(1, -1)
    in_hbm = {2, 5, 13, 14, 16}
    operands = (
        x, c, w_ada, b_ada, row2(norm_g[l]), jnp.swapaxes(w_in, 1, 2), row2(a_ln_g[l]), row2(a_ln_b[l]),
        a_w_s[l], a_b_s[l], b_w_gate2[l], row2(b_b_gate[l]), row2(b_norm_g[l]), w_proj_a, w_proj_b,
        row2(b_merge[l]), w_out, row2(final_g),
    )
    in_specs = [pl.BlockSpec((1, TILE_M, d), lambda b, j: (b, j, 0))] + [
        pl.BlockSpec(memory_space=pl.ANY) if i in in_hbm else _const_spec(op.shape)
        for i, op in enumerate(operands) if i >= 1]

    return pl.pallas_call(
        _block_kernel,
        grid=(bsz, seq // TILE_M),
        in_specs=in_specs,
        out_specs=pl.BlockSpec((1, TILE_M, d), lambda b, j: (b, j, 0)),
        out_shape=jax.ShapeDtypeStruct((bsz, seq, d), x.dtype),
        scratch_shapes=[pltpu.VMEM((B_HEADS, B_HEAD_K, B_HEAD_V), F32),
                        pltpu.VMEM((TILE_M // GLA_CHUNK * B_HEADS, GLA_CHUNK, GLA_CHUNK), BF16),
                        pltpu.VMEM((d, W_IN_RESIDENT), BF16),
                        pltpu.VMEM((3, d, W_SQ_RESIDENT), BF16),
                        pltpu.VMEM((SUBLANES, 3 * d), F32),
                        pltpu.VMEM((A_GROUPS, A_CHUNK, A_GROUP_DIM), F32),
                        pltpu.VMEM((2, STAGE_ROWS, D_MODEL), F32),
                        pltpu.SemaphoreType.DMA((2,))],
        compiler_params=pltpu.CompilerParams(
            dimension_semantics=("arbitrary", "arbitrary"),
            vmem_limit_bytes=VMEM_LIMIT_BYTES),
        name="hybrid_block",
    )(*operands)
```

```python
import math

import jax
import jax.numpy as jnp
from jax import lax
from jax.experimental import pallas as pl
from jax.experimental.pallas import tpu as pltpu

D_MODEL = 1024
A_GROUPS = 8
A_CHUNK = 128
A_GROUP_DIM = D_MODEL // A_GROUPS
B_HEADS = 4
B_KEY_DIM = D_MODEL // 2
B_VAL_DIM = D_MODEL
B_HEAD_K = B_KEY_DIM // B_HEADS
B_HEAD_V = B_VAL_DIM // B_HEADS
B_GATE_RANK = 16
B_GATE_NORMALIZER = 16.0
EPS = 1e-6

LANES = 128
SUBLANES = 8
GLA_CHUNK = 256
TILE_M = 512
VMEM_LIMIT_BYTES = 60 * 1024 * 1024

OFF_U, OFF_V, OFF_Z = 0, 1024, 2048
OFF_Q, OFF_K = 3072, 3584
OFF_BV, OFF_BZ = 4096, 5120
W_MAIN = 6 * D_MODEL
IN_WIDTH = W_MAIN + B_GATE_RANK + 2 * D_MODEL
OFF_G1 = W_MAIN
OFF_MA, OFF_MB = W_MAIN + LANES, W_MAIN + LANES + D_MODEL
W_IN_RESIDENT = OFF_MB + D_MODEL
W_SQ_RESIDENT = D_MODEL + LANES
STAGE_ROWS = 256
STAGE_SLOTS = 4

F32 = jnp.float32
BF16 = jnp.bfloat16
LOG2E = 1.4426950408889634
FAST_SPAN_LOG2 = 80.0


def _dot(a, b):
    return jnp.dot(a, b, preferred_element_type=F32)


def _dot_nt(a, b):
    return lax.dot_general(a, b, (((1,), (1,)), ((), ())), preferred_element_type=F32)


def _dot_tn(a, b):
    return lax.dot_general(a, b, (((0,), (0,)), ((), ())), preferred_element_type=F32)


def _split_bf16(a):
    hi = a.astype(BF16)
    lo = (a - hi.astype(F32)).astype(BF16)
    return hi, lo


def _gelu(a):
    return 0.5 * a * (1.0 + lax.erf(a * (1.0 / math.sqrt(2.0))))


def _silu(a):
    return a * jax.nn.sigmoid(a)


def _rms(a):
    return a * lax.rsqrt(jnp.mean(a * a, axis=-1, keepdims=True) + EPS)


def _bcast_row_in_blocks(a, blk, row):
    m, n = a.shape
    a3 = a.reshape(m // blk, blk, n)
    return jnp.broadcast_to(a3[:, row:row + 1, :], (m // blk, blk, n)).reshape(m, n)


def _scores_single_reference(qs, k, b):
    c = qs.shape[0]
    ref_b = b[c // 2 - 1:c // 2, :]
    s = _dot_nt((qs * jnp.exp2(b - ref_b)).astype(BF16), (k * jnp.exp2(ref_b - b)).astype(BF16))
    causal = (lax.broadcasted_iota(jnp.int32, (c, c), 0) >= lax.broadcasted_iota(jnp.int32, (c, c), 1))
    return jnp.where(causal, s, 0.0).astype(BF16)


def _scores_hierarchical(qs, k, b):
    c = qs.shape[0]
    row = lax.broadcasted_iota(jnp.int32, (c, 1), 0)
    col = lax.broadcasted_iota(jnp.int32, (1, c), 1)
    scores = jnp.zeros((c, c), F32)
    m = SUBLANES
    while m < c:
        ref_b = _bcast_row_in_blocks(b, 2 * m, m - 1)
        e = jnp.exp2(-jnp.abs(b - ref_b))
        siblings = ((row // m) == (col // m) + 1) & ((col // m) % 2 == 0)
        scores = jnp.where(siblings, _dot_nt((qs * e).astype(BF16), (k * e).astype(BF16)), scores)
        m *= 2
    row_in = row % SUBLANES
    lane = lax.broadcasted_iota(jnp.int32, (1, LANES), 1)
    diag = jnp.zeros((c, LANES), F32)
    for j in range(SUBLANES):
        b_j = _bcast_row_in_blocks(b, SUBLANES, j)
        k_j = _bcast_row_in_blocks(k, SUBLANES, j)
        s_j = jnp.sum(qs * k_j * jnp.exp2(jnp.minimum(b - b_j, 0.0)), axis=-1, keepdims=True)
        diag = jnp.where((lane == j) & (row_in >= j), s_j, diag)
    expand = (lax.broadcasted_iota(jnp.int32, (LANES, c), 0)
              == lax.broadcasted_iota(jnp.int32, (LANES, c), 1) % SUBLANES).astype(BF16)
    diag_full = _dot(diag.astype(BF16), expand)
    same_block = (row // SUBLANES) == (col // SUBLANES)
    return jnp.where(same_block, diag_full.astype(BF16), scores.astype(BF16))


def _gla_inter_chunk(qs, k, v_b, b, state_ref, h):
    c = qs.shape[0]
    state = state_ref[h]
    o = _dot((qs * jnp.exp2(b)).astype(BF16), state.astype(BF16))
    b_last = b[c - 1:c, :]
    k_dec = (k * jnp.exp2(b_last - b)).astype(BF16)
    decay_col = jnp.transpose(jnp.broadcast_to(jnp.exp2(b_last), (LANES, B_HEAD_K)))[:, 0:1]
    state_ref[h] = decay_col * state + _dot_tn(k_dec, v_b)
    return o


def _stage_weights(w_in_t_hbm, sq_hbm, w_ada_hbm, c_ref, b_ada_ref, w_in_s, w_sq_s, mod_s, stage_ref, sem_ref):
    tasks = []

    bsz = c_ref.shape[0]
    act = _silu(c_ref[...])
    act = jnp.concatenate([act, jnp.zeros((SUBLANES - bsz, D_MODEL), F32)], axis=0)
    mod_s[...] = jnp.broadcast_to(b_ada_ref[...], mod_s.shape)
    for c0 in range(0, 3 * D_MODEL, D_MODEL):
        for r0 in range(0, D_MODEL, STAGE_ROWS):
            def consume(slot, c0=c0, r0=r0):
                a_hi, a_lo = _split_bf16(act[:, r0:r0 + STAGE_ROWS])
                w_hi, w_lo = _split_bf16(stage_ref[slot])
                mod_s[:, c0:c0 + D_MODEL] += _dot(a_hi, w_hi) + (_dot(a_lo, w_hi) + _dot(a_hi, w_lo))
            tasks.append((w_ada_hbm.at[0, pl.ds(r0, STAGE_ROWS), pl.ds(c0, D_MODEL)], STAGE_ROWS, consume))

    def in_task(src_row0, rows, dst_col0, valid):
        def consume(slot):
            blk = jnp.transpose(stage_ref[slot, 0:rows, :])
            if valid < rows:
                blk = jnp.where(lax.broadcasted_iota(jnp.int32, blk.shape, 1) < valid, blk, 0.0)
            w_in_s[:, dst_col0:dst_col0 + rows] = blk.astype(BF16)
        tasks.append((w_in_t_hbm.at[0, pl.ds(src_row0, rows), :], rows, consume))

    for c0 in range(0, W_MAIN, STAGE_ROWS):
        in_task(c0, STAGE_ROWS, c0, STAGE_ROWS)
    in_task(W_MAIN, LANES, OFF_G1, B_GATE_RANK)
    for c0 in range(0, 2 * D_MODEL, STAGE_ROWS):
        in_task(W_MAIN + B_GATE_RANK + c0, STAGE_ROWS, OFF_MA + c0, STAGE_ROWS)
    for m, w_hbm in enumerate(sq_hbm):
        for r0 in range(0, D_MODEL, STAGE_ROWS):
            def consume(slot, m=m, r0=r0):
                w_sq_s[m, r0:r0 + STAGE_ROWS, 0:D_MODEL] = stage_ref[slot].astype(BF16)
            tasks.append((w_hbm.at[0, pl.ds(r0, STAGE_ROWS), :], STAGE_ROWS, consume))

    copies = [pltpu.make_async_copy(src, stage_ref.at[i % STAGE_SLOTS, pl.ds(0, rows), :],
                                    sem_ref.at[i % STAGE_SLOTS])
              for i, (src, rows, _) in enumerate(tasks)]
    ahead = STAGE_SLOTS - 1
    for cp in copies[:ahead]:
        cp.start()
    for i, (_, _, consume) in enumerate(tasks):
        if i + ahead < len(tasks):
            copies[i + ahead].start()
        copies[i].wait()
        consume(i % STAGE_SLOTS)


def _block_kernel(x_ref, c_ref, w_ada_hbm, b_ada_ref, ng_ref, w_in_hbm, alng_ref, alnb_ref, aws_ref, abs_row_ref,
                  wg2_ref, bbg_ref, bng_ref, wpa_hbm, wpb_hbm, bm_ref, wout_hbm, fg_ref,
                  o_ref, state_ref, scores_ref, w_in_ref, w_sq_ref, mod_ref, abs_ref, stage_ref, sem_ref):
    @pl.when((pl.program_id(0) == 0) & (pl.program_id(1) == 0))
    def _():
        _stage_weights(w_in_hbm, (wpa_hbm, wpb_hbm, wout_hbm), w_ada_hbm, c_ref, b_ada_ref,
                       w_in_ref, w_sq_ref, mod_ref, stage_ref, sem_ref)
        for g in range(A_GROUPS):
            abs_ref[g] = jnp.transpose(jnp.broadcast_to(abs_row_ref[g:g + 1, :], (A_GROUP_DIM, A_CHUNK)))

    @pl.when(pl.program_id(1) == 0)
    def _():
        state_ref[...] = jnp.zeros_like(state_ref)

    tm = x_ref.shape[1]
    x = x_ref[0]
    mod = mod_ref[pl.ds(pl.program_id(0), 1), :]
    shift = mod[:, 0:D_MODEL]
    scale = mod[:, D_MODEL:2 * D_MODEL]
    gate = mod[:, 2 * D_MODEL:3 * D_MODEL]

    hn_b = (_rms(x) * ng_ref[...] * (1.0 + scale) + shift).astype(BF16)

    def proj(lo, width, w_ref=w_in_ref):
        return _dot(hn_b, w_ref[:, lo:lo + width])

    g1 = proj(OFF_G1, LANES)
    q_all = proj(OFF_Q, B_KEY_DIM) * (B_HEAD_K ** -0.5)
    k_all = proj(OFF_K, B_KEY_DIM)
    wg2 = jnp.concatenate([wg2_ref[...], jnp.zeros((LANES - B_GATE_RANK, B_KEY_DIM), F32)], axis=0)
    gk = _dot(g1.astype(BF16), wg2.astype(BF16)) + bbg_ref[...]
    p_bv = proj(OFF_BV, B_VAL_DIM)
    log2_a = (jnp.minimum(gk, 0.0) - jnp.log1p(jnp.exp(-jnp.abs(gk)))) * (LOG2E / B_GATE_NORMALIZER)
    cc = GLA_CHUNK
    gla_chunks = tm // cc
    tril_b = (lax.broadcasted_iota(jnp.int32, (cc, cc), 0)
              >= lax.broadcasted_iota(jnp.int32, (cc, cc), 1)).astype(BF16)
    la_hi, la_lo = _split_bf16(log2_a)
    p_u = proj(OFF_U, D_MODEL)
    b_chunks = [_dot(tril_b, la_hi[ci * cc:(ci + 1) * cc]) + _dot(tril_b, la_lo[ci * cc:(ci + 1) * cc])
                for ci in range(gla_chunks)]
    v_all = p_bv.astype(BF16)

    head_args = []
    for ci in range(gla_chunks):
        rs = slice(ci * cc, (ci + 1) * cc)
        for h in range(B_HEADS):
            ks = slice(h * B_HEAD_K, (h + 1) * B_HEAD_K)
            head_args.append((q_all[rs, ks], k_all[rs, ks], b_chunks[ci][:, ks],
                              v_all[rs, h * B_HEAD_V:(h + 1) * B_HEAD_V]))

    p_v = proj(OFF_V, D_MODEL)
    o_inter = [_gla_inter_chunk(qs, k, v_b, b, state_ref, i % B_HEADS)
               for i, (qs, k, b, v_b) in enumerate(head_args)]

    p_z = proj(OFF_Z, D_MODEL)
    a_u = _gelu(p_u)
    a_v = _gelu(p_v)
    p_ma = proj(OFF_MA, D_MODEL)
    a_vc = a_v - jnp.mean(a_v, axis=-1, keepdims=True)
    a_vn = (a_vc * lax.rsqrt(jnp.mean(a_vc * a_vc, axis=-1, keepdims=True) + EPS)
            * alng_ref[...] + alnb_ref[...]).astype(BF16)

    tri = (lax.broadcasted_iota(jnp.int32, (A_CHUNK, A_CHUNK), 0)
           >= lax.broadcasted_iota(jnp.int32, (A_CHUNK, A_CHUNK), 1))
    n_chunks = tm // A_CHUNK
    mixed_g = []
    for g in range(A_GROUPS):
        cols = slice(g * A_GROUP_DIM, (g + 1) * A_GROUP_DIM)
        v_g = jnp.concatenate([a_vn[ci * A_CHUNK:(ci + 1) * A_CHUNK, cols] for ci in range(n_chunks)], axis=1)
        m_g = _dot(jnp.where(tri, aws_ref[g], 0.0).astype(BF16), v_g)
        mixed_g.append(jnp.concatenate(
            [m_g[:, ci * A_GROUP_DIM:(ci + 1) * A_GROUP_DIM] for ci in range(n_chunks)], axis=0) + jnp.concatenate(
            [abs_ref[g]] * n_chunks, axis=0))
    mixed = jnp.concatenate(mixed_g, axis=1)
    y_a = a_u * mixed * _silu(p_z)
    g_a = jax.nn.sigmoid(p_ma + bm_ref[:, 0:D_MODEL])
    merged = g_a * _dot(y_a.astype(BF16), w_sq_ref[0, :, 0:D_MODEL])

    for i, (qs, k, b, _) in enumerate(head_args):
        scores_ref[i] = _scores_single_reference(qs, k, b)
    span = jnp.zeros((1, B_KEY_DIM), F32)
    for b in b_chunks:
        b_mid = b[cc // 2 - 1:cc // 2, :]
        span = jnp.maximum(span, jnp.maximum(-b_mid, b_mid - b[cc - 1:cc, :]))
    wide_span = jnp.max(span) >= FAST_SPAN_LOG2

    @pl.when(wide_span)
    def _():
        for i, (qs, k, b, _) in enumerate(head_args):
            scores_ref[i] = _scores_hierarchical(qs, k, b)

    p_bz = proj(OFF_BZ, B_VAL_DIM)
    heads = []
    for i, (_, _, _, v_b) in enumerate(head_args):
        o_h = o_inter[i] + _dot(scores_ref[i], v_b)
        heads.append(_rms(o_h) * bng_ref[...])
    p_mb = proj(OFF_MB, D_MODEL)
    o_attn = jnp.concatenate([jnp.concatenate(heads[ci * B_HEADS:(ci + 1) * B_HEADS], axis=1)
                              for ci in range(gla_chunks)], axis=0)
    o_b = o_attn * _silu(p_bz)
    y_b = _dot(o_b.astype(BF16), w_sq_ref[1, :, 0:D_MODEL])
    g_b = jax.nn.sigmoid(p_mb + bm_ref[:, D_MODEL:2 * D_MODEL])
    merged = merged + g_b * y_b

    out = _dot(merged.astype(BF16), w_sq_ref[2, :, 0:D_MODEL])
    o_ref[0] = _rms(x + gate * out) * fg_ref[...]


def _const_spec(shape):
    nd = len(shape)
    return pl.BlockSpec(shape, lambda b, j: (0,) * nd, pipeline_mode=pl.Buffered(1))


def kernel(x, c, w_ada, b_ada, norm_g, w_in, a_ln_g, a_ln_b, a_w_s, a_b_s, b_w_gate2, b_b_gate, b_norm_g, w_proj_a, w_proj_b, b_merge, w_out, final_g):
    bsz, seq, d = x.shape
    assert d == D_MODEL and seq % TILE_M == 0 and w_ada.shape[0] == 1 and bsz <= SUBLANES
    l = 0

    row2 = lambda v: v.reshape(1, -1)
    in_hbm = {2, 5, 13, 14, 16}
    operands = (
        x, c, w_ada, b_ada, row2(norm_g[l]), jnp.swapaxes(w_in, 1, 2), row2(a_ln_g[l]), row2(a_ln_b[l]),
        a_w_s[l], a_b_s[l], b_w_gate2[l], row2(b_b_gate[l]), row2(b_norm_g[l]), w_proj_a, w_proj_b,
        row2(b_merge[l]), w_out, row2(final_g),
    )
    in_specs = [pl.BlockSpec((1, TILE_M, d), lambda b, j: (b, j, 0))] + [
        pl.BlockSpec(memory_space=pl.ANY) if i in in_hbm else _const_spec(op.shape)
        for i, op in enumerate(operands) if i >= 1]

    return pl.pallas_call(
        _block_kernel,
        grid=(bsz, seq // TILE_M),
        in_specs=in_specs,
        out_specs=pl.BlockSpec((1, TILE_M, d), lambda b, j: (b, j, 0)),
        out_shape=jax.ShapeDtypeStruct((bsz, seq, d), x.dtype),
        scratch_shapes=[pltpu.VMEM((B_HEADS, B_HEAD_K, B_HEAD_V), F32),
                        pltpu.VMEM((TILE_M // GLA_CHUNK * B_HEADS, GLA_CHUNK, GLA_CHUNK), BF16),
                        pltpu.VMEM((d, W_IN_RESIDENT), BF16),
                        pltpu.VMEM((3, d, W_SQ_RESIDENT), BF16),
                        pltpu.VMEM((SUBLANES, 3 * d), F32),
                        pltpu.VMEM((A_GROUPS, A_CHUNK, A_GROUP_DIM), F32),
                        pltpu.VMEM((STAGE_SLOTS, STAGE_ROWS, D_MODEL), F32),
                        pltpu.SemaphoreType.DMA((STAGE_SLOTS,))],
        compiler_params=pltpu.CompilerParams(
            dimension_semantics=("arbitrary", "arbitrary"),
            vmem_limit_bytes=VMEM_LIMIT_BYTES),
        name="hybrid_block",
    )(*operands)
```

```python
import math

import jax
import jax.numpy as jnp
from jax import lax
from jax.experimental import pallas as pl
from jax.experimental.pallas import tpu as pltpu

D_MODEL = 1024
A_GROUPS = 8
A_CHUNK = 128
A_GROUP_DIM = D_MODEL // A_GROUPS
B_HEADS = 4
B_KEY_DIM = D_MODEL // 2
B_VAL_DIM = D_MODEL
B_HEAD_K = B_KEY_DIM // B_HEADS
B_HEAD_V = B_VAL_DIM // B_HEADS
B_GATE_RANK = 16
B_GATE_NORMALIZER = 16.0
EPS = 1e-6

LANES = 128
SUBLANES = 8
GLA_CHUNK = 256
TILE_M = 512
VMEM_LIMIT_BYTES = 60 * 1024 * 1024

OFF_U, OFF_V, OFF_Z = 0, 1024, 2048
OFF_Q, OFF_K = 3072, 3584
OFF_BV, OFF_BZ = 4096, 5120
W_MAIN = 6 * D_MODEL
IN_WIDTH = W_MAIN + B_GATE_RANK + 2 * D_MODEL
OFF_G1 = W_MAIN
OFF_MA, OFF_MB = W_MAIN + LANES, W_MAIN + LANES + D_MODEL
W_IN_RESIDENT = OFF_MB + D_MODEL
W_SQ_RESIDENT = D_MODEL + LANES
STAGE_ROWS = 128
STAGE_SLOTS = 8

F32 = jnp.float32
BF16 = jnp.bfloat16
LOG2E = 1.4426950408889634
FAST_SPAN_LOG2 = 80.0


def _dot(a, b):
    return jnp.dot(a, b, preferred_element_type=F32)


def _dot_nt(a, b):
    return lax.dot_general(a, b, (((1,), (1,)), ((), ())), preferred_element_type=F32)


def _dot_tn(a, b):
    return lax.dot_general(a, b, (((0,), (0,)), ((), ())), preferred_element_type=F32)


def _split_bf16(a):
    hi = a.astype(BF16)
    lo = (a - hi.astype(F32)).astype(BF16)
    return hi, lo


def _gelu(a):
    return 0.5 * a * (1.0 + lax.erf(a * (1.0 / math.sqrt(2.0))))


def _silu(a):
    return a * jax.nn.sigmoid(a)


def _rms(a):
    return a * lax.rsqrt(jnp.mean(a * a, axis=-1, keepdims=True) + EPS)


def _bcast_row_in_blocks(a, blk, row):
    m, n = a.shape
    a3 = a.reshape(m // blk, blk, n)
    return jnp.broadcast_to(a3[:, row:row + 1, :], (m // blk, blk, n)).reshape(m, n)


def _scores_single_reference(qs, k, b):
    c = qs.shape[0]
    ref_b = b[c // 2 - 1:c // 2, :]
    s = _dot_nt((qs * jnp.exp2(b - ref_b)).astype(BF16), (k * jnp.exp2(ref_b - b)).astype(BF16))
    causal = (lax.broadcasted_iota(jnp.int32, (c, c), 0) >= lax.broadcasted_iota(jnp.int32, (c, c), 1))
    return jnp.where(causal, s, 0.0).astype(BF16)


def _scores_hierarchical(qs, k, b):
    c = qs.shape[0]
    row = lax.broadcasted_iota(jnp.int32, (c, 1), 0)
    col = lax.broadcasted_iota(jnp.int32, (1, c), 1)
    scores = jnp.zeros((c, c), F32)
    m = SUBLANES
    while m < c:
        ref_b = _bcast_row_in_blocks(b, 2 * m, m - 1)
        e = jnp.exp2(-jnp.abs(b - ref_b))
        siblings = ((row // m) == (col // m) + 1) & ((col // m) % 2 == 0)
        scores = jnp.where(siblings, _dot_nt((qs * e).astype(BF16), (k * e).astype(BF16)), scores)
        m *= 2
    row_in = row % SUBLANES
    lane = lax.broadcasted_iota(jnp.int32, (1, LANES), 1)
    diag = jnp.zeros((c, LANES), F32)
    for j in range(SUBLANES):
        b_j = _bcast_row_in_blocks(b, SUBLANES, j)
        k_j = _bcast_row_in_blocks(k, SUBLANES, j)
        s_j = jnp.sum(qs * k_j * jnp.exp2(jnp.minimum(b - b_j, 0.0)), axis=-1, keepdims=True)
        diag = jnp.where((lane == j) & (row_in >= j), s_j, diag)
    expand = (lax.broadcasted_iota(jnp.int32, (LANES, c), 0)
              == lax.broadcasted_iota(jnp.int32, (LANES, c), 1) % SUBLANES).astype(BF16)
    diag_full = _dot(diag.astype(BF16), expand)
    same_block = (row // SUBLANES) == (col // SUBLANES)
    return jnp.where(same_block, diag_full.astype(BF16), scores.astype(BF16))


def _gla_inter_chunk(qs, k, v_b, b, state_ref, h):
    c = qs.shape[0]
    state = state_ref[h]
    o = _dot((qs * jnp.exp2(b)).astype(BF16), state.astype(BF16))
    b_last = b[c - 1:c, :]
    k_dec = (k * jnp.exp2(b_last - b)).astype(BF16)
    decay_col = jnp.transpose(jnp.broadcast_to(jnp.exp2(b_last), (LANES, B_HEAD_K)))[:, 0:1]
    state_ref[h] = decay_col * state + _dot_tn(k_dec, v_b)
    return o


def _stage_weights(w_in_t_hbm, sq_hbm, w_ada_hbm, c_ref, b_ada_ref, w_in_s, w_sq_s, mod_s, stage_ref, sem_ref):
    tasks = []

    bsz = c_ref.shape[0]
    act = _silu(c_ref[...])
    act = jnp.concatenate([act, jnp.zeros((SUBLANES - bsz, D_MODEL), F32)], axis=0)
    mod_s[...] = jnp.broadcast_to(b_ada_ref[...], mod_s.shape)
    for c0 in range(0, 3 * D_MODEL, D_MODEL):
        for r0 in range(0, D_MODEL, STAGE_ROWS):
            def consume(slot, c0=c0, r0=r0):
                a_hi, a_lo = _split_bf16(act[:, r0:r0 + STAGE_ROWS])
                w_hi, w_lo = _split_bf16(stage_ref[slot])
                mod_s[:, c0:c0 + D_MODEL] += _dot(a_hi, w_hi) + (_dot(a_lo, w_hi) + _dot(a_hi, w_lo))
            tasks.append((w_ada_hbm.at[0, pl.ds(r0, STAGE_ROWS), pl.ds(c0, D_MODEL)], STAGE_ROWS, consume))

    def in_task(src_row0, rows, dst_col0, valid):
        def consume(slot):
            blk = jnp.transpose(stage_ref[slot, 0:rows, :])
            if valid < rows:
                blk = jnp.where(lax.broadcasted_iota(jnp.int32, blk.shape, 1) < valid, blk, 0.0)
            w_in_s[:, dst_col0:dst_col0 + rows] = blk.astype(BF16)
        tasks.append((w_in_t_hbm.at[0, pl.ds(src_row0, rows), :], rows, consume))

    for c0 in range(0, W_MAIN, STAGE_ROWS):
        in_task(c0, STAGE_ROWS, c0, STAGE_ROWS)
    in_task(W_MAIN, LANES, OFF_G1, B_GATE_RANK)
    for c0 in range(0, 2 * D_MODEL, STAGE_ROWS):
        in_task(W_MAIN + B_GATE_RANK + c0, STAGE_ROWS, OFF_MA + c0, STAGE_ROWS)
    for m, w_hbm in enumerate(sq_hbm):
        for r0 in range(0, D_MODEL, STAGE_ROWS):
            def consume(slot, m=m, r0=r0):
                w_sq_s[m, r0:r0 + STAGE_ROWS, 0:D_MODEL] = stage_ref[slot].astype(BF16)
            tasks.append((w_hbm.at[0, pl.ds(r0, STAGE_ROWS), :], STAGE_ROWS, consume))

    copies = [pltpu.make_async_copy(src, stage_ref.at[i % STAGE_SLOTS, pl.ds(0, rows), :],
                                    sem_ref.at[i % STAGE_SLOTS])
              for i, (src, rows, _) in enumerate(tasks)]
    ahead = STAGE_SLOTS - 1
    for cp in copies[:ahead]:
        cp.start()
    for i, (_, _, consume) in enumerate(tasks):
        if i + ahead < len(tasks):
            copies[i + ahead].start()
        copies[i].wait()
        consume(i % STAGE_SLOTS)


def _block_kernel(x_ref, c_ref, w_ada_hbm, b_ada_ref, ng_ref, w_in_hbm, alng_ref, alnb_ref, aws_ref, abs_row_ref,
                  wg2_ref, bbg_ref, bng_ref, wpa_hbm, wpb_hbm, bm_ref, wout_hbm, fg_ref,
                  o_ref, state_ref, scores_ref, w_in_ref, w_sq_ref, mod_ref, abs_ref, stage_ref, sem_ref):
    @pl.when((pl.program_id(0) == 0) & (pl.program_id(1) == 0))
    def _():
        _stage_weights(w_in_hbm, (wpa_hbm, wpb_hbm, wout_hbm), w_ada_hbm, c_ref, b_ada_ref,
                       w_in_ref, w_sq_ref, mod_ref, stage_ref, sem_ref)
        for g in range(A_GROUPS):
            abs_ref[g] = jnp.transpose(jnp.broadcast_to(abs_row_ref[g:g + 1, :], (A_GROUP_DIM, A_CHUNK)))

    @pl.when(pl.program_id(1) == 0)
    def _():
        state_ref[...] = jnp.zeros_like(state_ref)

    tm = x_ref.shape[1]
    x = x_ref[0]
    mod = mod_ref[pl.ds(pl.program_id(0), 1), :]
    shift = mod[:, 0:D_MODEL]
    scale = mod[:, D_MODEL:2 * D_MODEL]
    gate = mod[:, 2 * D_MODEL:3 * D_MODEL]

    hn_b = (_rms(x) * ng_ref[...] * (1.0 + scale) + shift).astype(BF16)

    def proj(lo, width, w_ref=w_in_ref):
        return _dot(hn_b, w_ref[:, lo:lo + width])

    g1 = proj(OFF_G1, LANES)
    q_all = proj(OFF_Q, B_KEY_DIM) * (B_HEAD_K ** -0.5)
    k_all = proj(OFF_K, B_KEY_DIM)
    wg2 = jnp.concatenate([wg2_ref[...], jnp.zeros((LANES - B_GATE_RANK, B_KEY_DIM), F32)], axis=0)
    gk = _dot(g1.astype(BF16), wg2.astype(BF16)) + bbg_ref[...]
    p_bv = proj(OFF_BV, B_VAL_DIM)
    log2_a = (jnp.minimum(gk, 0.0) - jnp.log1p(jnp.exp(-jnp.abs(gk)))) * (LOG2E / B_GATE_NORMALIZER)
    cc = GLA_CHUNK
    gla_chunks = tm // cc
    tril_b = (lax.broadcasted_iota(jnp.int32, (cc, cc), 0)
              >= lax.broadcasted_iota(jnp.int32, (cc, cc), 1)).astype(BF16)
    la_hi, la_lo = _split_bf16(log2_a)
    p_u = proj(OFF_U, D_MODEL)
    b_chunks = [_dot(tril_b, la_hi[ci * cc:(ci + 1) * cc]) + _dot(tril_b, la_lo[ci * cc:(ci + 1) * cc])
                for ci in range(gla_chunks)]
    v_all = p_bv.astype(BF16)

    head_args = []
    for ci in range(gla_chunks):
        rs = slice(ci * cc, (ci + 1) * cc)
        for h in range(B_HEADS):
            ks = slice(h * B_HEAD_K, (h + 1) * B_HEAD_K)
            head_args.append((q_all[rs, ks], k_all[rs, ks], b_chunks[ci][:, ks],
                              v_all[rs, h * B_HEAD_V:(h + 1) * B_HEAD_V]))

    p_v = proj(OFF_V, D_MODEL)
    o_inter = [_gla_inter_chunk(qs, k, v_b, b, state_ref, i % B_HEADS)
               for i, (qs, k, b, v_b) in enumerate(head_args)]

    p_z = proj(OFF_Z, D_MODEL)
    a_u = _gelu(p_u)
    a_v = _gelu(p_v)
    p_ma = proj(OFF_MA, D_MODEL)
    a_vc = a_v - jnp.mean(a_v, axis=-1, keepdims=True)
    a_vn = (a_vc * lax.rsqrt(jnp.mean(a_vc * a_vc, axis=-1, keepdims=True) + EPS)
            * alng_ref[...] + alnb_ref[...]).astype(BF16)

    tri = (lax.broadcasted_iota(jnp.int32, (A_CHUNK, A_CHUNK), 0)
           >= lax.broadcasted_iota(jnp.int32, (A_CHUNK, A_CHUNK), 1))
    n_chunks = tm // A_CHUNK
    mixed_g = []
    for g in range(A_GROUPS):
        cols = slice(g * A_GROUP_DIM, (g + 1) * A_GROUP_DIM)
        v_g = jnp.concatenate([a_vn[ci * A_CHUNK:(ci + 1) * A_CHUNK, cols] for ci in range(n_chunks)], axis=1)
        m_g = _dot(jnp.where(tri, aws_ref[g], 0.0).astype(BF16), v_g)
        mixed_g.append(jnp.concatenate(
            [m_g[:, ci * A_GROUP_DIM:(ci + 1) * A_GROUP_DIM] for ci in range(n_chunks)], axis=0) + jnp.concatenate(
            [abs_ref[g]] * n_chunks, axis=0))
    mixed = jnp.concatenate(mixed_g, axis=1)
    y_a = a_u * mixed * _silu(p_z)
    g_a = jax.nn.sigmoid(p_ma + bm_ref[:, 0:D_MODEL])
    merged = g_a * _dot(y_a.astype(BF16), w_sq_ref[0, :, 0:D_MODEL])

    for i, (qs, k, b, _) in enumerate(head_args):
        scores_ref[i] = _scores_single_reference(qs, k, b)
    span = jnp.zeros((1, B_KEY_DIM), F32)
    for b in b_chunks:
        b_mid = b[cc // 2 - 1:cc // 2, :]
        span = jnp.maximum(span, jnp.maximum(-b_mid, b_mid - b[cc - 1:cc, :]))
    wide_span = jnp.max(span) >= FAST_SPAN_LOG2

    @pl.when(wide_span)
    def _():
        for i, (qs, k, b, _) in enumerate(head_args):
            scores_ref[i] = _scores_hierarchical(qs, k, b)

    p_bz = proj(OFF_BZ, B_VAL_DIM)
    heads = []
    for i, (_, _, _, v_b) in enumerate(head_args):
        o_h = o_inter[i] + _dot(scores_ref[i], v_b)
        heads.append(_rms(o_h) * bng_ref[...])
    p_mb = proj(OFF_MB, D_MODEL)
    o_attn = jnp.concatenate([jnp.concatenate(heads[ci * B_HEADS:(ci + 1) * B_HEADS], axis=1)
                              for ci in range(gla_chunks)], axis=0)
    o_b = o_attn * _silu(p_bz)
    y_b = _dot(o_b.astype(BF16), w_sq_ref[1, :, 0:D_MODEL])
    g_b = jax.nn.sigmoid(p_mb + bm_ref[:, D_MODEL:2 * D_MODEL])
    merged = merged + g_b * y_b

    out = _dot(merged.astype(BF16), w_sq_ref[2, :, 0:D_MODEL])
    o_ref[0] = _rms(x + gate * out) * fg_ref[...]


def _const_spec(shape):
    nd = len(shape)
    return pl.BlockSpec(shape, lambda b, j: (0,) * nd, pipeline_mode=pl.Buffered(1))


def kernel(x, c, w_ada, b_ada, norm_g, w_in, a_ln_g, a_ln_b, a_w_s, a_b_s, b_w_gate2, b_b_gate, b_norm_g, w_proj_a, w_proj_b, b_merge, w_out, final_g):
    bsz, seq, d = x.shape
    assert d == D_MODEL and seq % TILE_M == 0 and w_ada.shape[0] == 1 and bsz <= SUBLANES
    l = 0

    row2 = lambda v: v.reshape(1, -1)
    in_hbm = {2, 5, 13, 14, 16}
    operands = (
        x, c, w_ada, b_ada, row2(norm_g[l]), jnp.swapaxes(w_in, 1, 2), row2(a_ln_g[l]), row2(a_ln_b[l]),
        a_w_s[l], a_b_s[l], b_w_gate2[l], row2(b_b_gate[l]), row2(b_norm_g[l]), w_proj_a, w_proj_b,
        row2(b_merge[l]), w_out, row2(final_g),
    )
    in_specs = [pl.BlockSpec((1, TILE_M, d), lambda b, j: (b, j, 0))] + [
        pl.BlockSpec(memory_space=pl.ANY) if i in in_hbm else _const_spec(op.shape)
        for i, op in enumerate(operands) if i >= 1]

    return pl.pallas_call(
        _block_kernel,
        grid=(bsz, seq // TILE_M),
        in_specs=in_specs,
        out_specs=pl.BlockSpec((1, TILE_M, d), lambda b, j: (b, j, 0)),
        out_shape=jax.ShapeDtypeStruct((bsz, seq, d), x.dtype),
        scratch_shapes=[pltpu.VMEM((B_HEADS, B_HEAD_K, B_HEAD_V), F32),
                        pltpu.VMEM((TILE_M // GLA_CHUNK * B_HEADS, GLA_CHUNK, GLA_CHUNK), BF16),
                        pltpu.VMEM((d, W_IN_RESIDENT), BF16),
                        pltpu.VMEM((3, d, W_SQ_RESIDENT), BF16),
                        pltpu.VMEM((SUBLANES, 3 * d), F32),
                        pltpu.VMEM((A_GROUPS, A_CHUNK, A_GROUP_DIM), F32),
                        pltpu.VMEM((STAGE_SLOTS, STAGE_ROWS, D_MODEL), F32),
                        pltpu.SemaphoreType.DMA((STAGE_SLOTS,))],
        compiler_params=pltpu.CompilerParams(
            dimension_semantics=("arbitrary", "arbitrary"),
            vmem_limit_bytes=VMEM_LIMIT_BYTES),
        name="hybrid_block",
    )(*operands)
```

```python
import math

import jax
import jax.numpy as jnp
from jax import lax
from jax.experimental import pallas as pl
from jax.experimental.pallas import tpu as pltpu

D_MODEL = 1024
A_GROUPS = 8
A_CHUNK = 128
A_GROUP_DIM = D_MODEL // A_GROUPS
B_HEADS = 4
B_KEY_DIM = D_MODEL // 2
B_VAL_DIM = D_MODEL
B_HEAD_K = B_KEY_DIM // B_HEADS
B_HEAD_V = B_VAL_DIM // B_HEADS
B_GATE_RANK = 16
B_GATE_NORMALIZER = 16.0
EPS = 1e-6

LANES = 128
SUBLANES = 8
GLA_CHUNK = 256
TILE_M = 512
VMEM_LIMIT_BYTES = 60 * 1024 * 1024

OFF_U, OFF_V, OFF_Z = 0, 1024, 2048
OFF_Q, OFF_K = 3072, 3584
OFF_BV, OFF_BZ = 4096, 5120
W_MAIN = 6 * D_MODEL
IN_WIDTH = W_MAIN + B_GATE_RANK + 2 * D_MODEL
OFF_G1 = W_MAIN
OFF_MA, OFF_MB = W_MAIN + LANES, W_MAIN + LANES + D_MODEL
W_IN_RESIDENT = OFF_MB + D_MODEL
W_SQ_RESIDENT = D_MODEL + LANES
STAGE_ROWS = 128
STAGE_SLOTS = 8

F32 = jnp.float32
BF16 = jnp.bfloat16
LOG2E = 1.4426950408889634
FAST_SPAN_LOG2 = 80.0


def _dot(a, b):
    return jnp.dot(a, b, preferred_element_type=F32)


def _dot_nt(a, b):
    return lax.dot_general(a, b, (((1,), (1,)), ((), ())), preferred_element_type=F32)


def _dot_tn(a, b):
    return lax.dot_general(a, b, (((0,), (0,)), ((), ())), preferred_element_type=F32)


def _split_bf16(a):
    hi = a.astype(BF16)
    lo = (a - hi.astype(F32)).astype(BF16)
    return hi, lo


def _gelu(a):
    return 0.5 * a * (1.0 + lax.erf(a * (1.0 / math.sqrt(2.0))))


def _silu(a):
    return a * jax.nn.sigmoid(a)


def _rms(a):
    return a * lax.rsqrt(jnp.mean(a * a, axis=-1, keepdims=True) + EPS)


def _bcast_row_in_blocks(a, blk, row):
    m, n = a.shape
    a3 = a.reshape(m // blk, blk, n)
    return jnp.broadcast_to(a3[:, row:row + 1, :], (m // blk, blk, n)).reshape(m, n)


def _scores_single_reference(qs, k, b):
    c = qs.shape[0]
    ref_b = b[c // 2 - 1:c // 2, :]
    s = _dot_nt((qs * jnp.exp2(b - ref_b)).astype(BF16), (k * jnp.exp2(ref_b - b)).astype(BF16))
    causal = (lax.broadcasted_iota(jnp.int32, (c, c), 0) >= lax.broadcasted_iota(jnp.int32, (c, c), 1))
    return jnp.where(causal, s, 0.0).astype(BF16)


def _scores_hierarchical(qs, k, b):
    c = qs.shape[0]
    row = lax.broadcasted_iota(jnp.int32, (c, 1), 0)
    col = lax.broadcasted_iota(jnp.int32, (1, c), 1)
    scores = jnp.zeros((c, c), F32)
    m = SUBLANES
    while m < c:
        ref_b = _bcast_row_in_blocks(b, 2 * m, m - 1)
        e = jnp.exp2(-jnp.abs(b - ref_b))
        siblings = ((row // m) == (col // m) + 1) & ((col // m) % 2 == 0)
        scores = jnp.where(siblings, _dot_nt((qs * e).astype(BF16), (k * e).astype(BF16)), scores)
        m *= 2
    row_in = row % SUBLANES
    lane = lax.broadcasted_iota(jnp.int32, (1, LANES), 1)
    diag = jnp.zeros((c, LANES), F32)
    for j in range(SUBLANES):
        b_j = _bcast_row_in_blocks(b, SUBLANES, j)
        k_j = _bcast_row_in_blocks(k, SUBLANES, j)
        s_j = jnp.sum(qs * k_j * jnp.exp2(jnp.minimum(b - b_j, 0.0)), axis=-1, keepdims=True)
        diag = jnp.where((lane == j) & (row_in >= j), s_j, diag)
    expand = (lax.broadcasted_iota(jnp.int32, (LANES, c), 0)
              == lax.broadcasted_iota(jnp.int32, (LANES, c), 1) % SUBLANES).astype(BF16)
    diag_full = _dot(diag.astype(BF16), expand)
    same_block = (row // SUBLANES) == (col // SUBLANES)
    return jnp.where(same_block, diag_full.astype(BF16), scores.astype(BF16))


def _gla_inter_chunk(qs, k, v_b, b, state_ref, h):
    c = qs.shape[0]
    state = state_ref[h]
    o = _dot((qs * jnp.exp2(b)).astype(BF16), state.astype(BF16))
    b_last = b[c - 1:c, :]
    k_dec = (k * jnp.exp2(b_last - b)).astype(BF16)
    decay_col = jnp.transpose(jnp.broadcast_to(jnp.exp2(b_last), (LANES, B_HEAD_K)))[:, 0:1]
    state_ref[h] = decay_col * state + _dot_tn(k_dec, v_b)
    return o


def _stage_weights(w_in_t_hbm, sq_hbm, w_ada_hbm, c_ref, b_ada_ref, w_in_s, w_sq_s, mod_s, stage_ref, sem_ref):
    tasks = []

    bsz = c_ref.shape[0]
    act = _silu(c_ref[...])
    act = jnp.concatenate([act, jnp.zeros((SUBLANES - bsz, D_MODEL), F32)], axis=0)
    mod_s[...] = jnp.broadcast_to(b_ada_ref[...], mod_s.shape)
    for c0 in range(0, 3 * D_MODEL, D_MODEL):
        for r0 in range(0, D_MODEL, STAGE_ROWS):
            def consume(slot, c0=c0, r0=r0):
                a_hi, a_lo = _split_bf16(act[:, r0:r0 + STAGE_ROWS])
                w_hi, w_lo = _split_bf16(stage_ref[slot])
                mod_s[:, c0:c0 + D_MODEL] += _dot(a_hi, w_hi) + (_dot(a_lo, w_hi) + _dot(a_hi, w_lo))
            tasks.append((w_ada_hbm.at[0, pl.ds(r0, STAGE_ROWS), pl.ds(c0, D_MODEL)], STAGE_ROWS, consume))

    def in_task(src_row0, rows, dst_col0, valid):
        def consume(slot):
            blk = jnp.transpose(stage_ref[slot, 0:rows, :])
            if valid < rows:
                blk = jnp.where(lax.broadcasted_iota(jnp.int32, blk.shape, 1) < valid, blk, 0.0)
            w_in_s[:, dst_col0:dst_col0 + rows] = blk.astype(BF16)
        tasks.append((w_in_t_hbm.at[0, pl.ds(src_row0, rows), :], rows, consume))

    for c0 in range(0, W_MAIN, STAGE_ROWS):
        in_task(c0, STAGE_ROWS, c0, STAGE_ROWS)
    in_task(W_MAIN, LANES, OFF_G1, B_GATE_RANK)
    for c0 in range(0, 2 * D_MODEL, STAGE_ROWS):
        in_task(W_MAIN + B_GATE_RANK + c0, STAGE_ROWS, OFF_MA + c0, STAGE_ROWS)
    for m, w_hbm in enumerate(sq_hbm):
        for r0 in range(0, D_MODEL, STAGE_ROWS):
            def consume(slot, m=m, r0=r0):
                w_sq_s[m, r0:r0 + STAGE_ROWS, 0:D_MODEL] = stage_ref[slot].astype(BF16)
            tasks.append((w_hbm.at[0, pl.ds(r0, STAGE_ROWS), :], STAGE_ROWS, consume))

    copies = [pltpu.make_async_copy(src, stage_ref.at[i % STAGE_SLOTS, pl.ds(0, rows), :],
                                    sem_ref.at[i % STAGE_SLOTS])
              for i, (src, rows, _) in enumerate(tasks)]
    ahead = STAGE_SLOTS - 1
    for cp in copies[:ahead]:
        cp.start()
    for i, (_, _, consume) in enumerate(tasks):
        if i + ahead < len(tasks):
            copies[i + ahead].start()
        copies[i].wait()
        consume(i % STAGE_SLOTS)


def _block_kernel(x_ref, c_ref, w_ada_hbm, b_ada_ref, ng_ref, w_in_hbm, alng_ref, alnb_ref, aws_ref, abs_row_ref,
                  wg2_ref, bbg_ref, bng_ref, wpa_hbm, wpb_hbm, bm_ref, wout_hbm, fg_ref,
                  o_ref, state_ref, scores_ref, w_in_ref, w_sq_ref, mod_ref, abs_ref, stage_ref, sem_ref):
    @pl.when((pl.program_id(0) == 0) & (pl.program_id(1) == 0))
    def _():
        _stage_weights(w_in_hbm, (wpa_hbm, wpb_hbm, wout_hbm), w_ada_hbm, c_ref, b_ada_ref,
                       w_in_ref, w_sq_ref, mod_ref, stage_ref, sem_ref)
        for g in range(A_GROUPS):
            abs_ref[g] = jnp.transpose(jnp.broadcast_to(abs_row_ref[g:g + 1, :], (A_GROUP_DIM, A_CHUNK)))

    @pl.when(pl.program_id(1) == 0)
    def _():
        state_ref[...] = jnp.zeros_like(state_ref)

    tm = x_ref.shape[1]
    x = x_ref[0]
    mod = mod_ref[pl.ds(pl.program_id(0), 1), :]
    shift = mod[:, 0:D_MODEL]
    scale = mod[:, D_MODEL:2 * D_MODEL]
    gate = mod[:, 2 * D_MODEL:3 * D_MODEL]

    hn_b = (_rms(x) * ng_ref[...] * (1.0 + scale) + shift).astype(BF16)

    def proj(lo, width, w_ref=w_in_ref):
        return _dot(hn_b, w_ref[:, lo:lo + width])

    g1 = proj(OFF_G1, LANES)
    q_all = proj(OFF_Q, B_KEY_DIM) * (B_HEAD_K ** -0.5)
    k_all = proj(OFF_K, B_KEY_DIM)
    wg2 = jnp.concatenate([wg2_ref[...], jnp.zeros((LANES - B_GATE_RANK, B_KEY_DIM), F32)], axis=0)
    gk = _dot(g1.astype(BF16), wg2.astype(BF16)) + bbg_ref[...]
    p_bv = proj(OFF_BV, B_VAL_DIM)
    log2_a = (jnp.minimum(gk, 0.0) - jnp.log1p(jnp.exp(-jnp.abs(gk)))) * (LOG2E / B_GATE_NORMALIZER)
    cc = GLA_CHUNK
    gla_chunks = tm // cc
    tril_b = (lax.broadcasted_iota(jnp.int32, (cc, cc), 0)
              >= lax.broadcasted_iota(jnp.int32, (cc, cc), 1)).astype(BF16)
    la_hi, la_lo = _split_bf16(log2_a)
    p_u = proj(OFF_U, D_MODEL)
    b_chunks = [_dot(tril_b, la_hi[ci * cc:(ci + 1) * cc]) + _dot(tril_b, la_lo[ci * cc:(ci + 1) * cc])
                for ci in range(gla_chunks)]
    v_all = p_bv.astype(BF16)

    head_args = []
    for ci in range(gla_chunks):
        rs = slice(ci * cc, (ci + 1) * cc)
        for h in range(B_HEADS):
            ks = slice(h * B_HEAD_K, (h + 1) * B_HEAD_K)
            head_args.append((q_all[rs, ks], k_all[rs, ks], b_chunks[ci][:, ks],
                              v_all[rs, h * B_HEAD_V:(h + 1) * B_HEAD_V]))

    p_v = proj(OFF_V, D_MODEL)
    o_inter = [_gla_inter_chunk(qs, k, v_b, b, state_ref, i % B_HEADS)
               for i, (qs, k, b, v_b) in enumerate(head_args)]

    p_z = proj(OFF_Z, D_MODEL)
    a_u = _gelu(p_u)
    a_v = _gelu(p_v)
    p_ma = proj(OFF_MA, D_MODEL)
    a_vc = a_v - jnp.mean(a_v, axis=-1, keepdims=True)
    a_vn = (a_vc * lax.rsqrt(jnp.mean(a_vc * a_vc, axis=-1, keepdims=True) + EPS)
            * alng_ref[...] + alnb_ref[...]).astype(BF16)

    tri = (lax.broadcasted_iota(jnp.int32, (A_CHUNK, A_CHUNK), 0)
           >= lax.broadcasted_iota(jnp.int32, (A_CHUNK, A_CHUNK), 1))
    n_chunks = tm // A_CHUNK
    mixed_g = []
    for g in range(A_GROUPS):
        cols = slice(g * A_GROUP_DIM, (g + 1) * A_GROUP_DIM)
        v_g = jnp.concatenate([a_vn[ci * A_CHUNK:(ci + 1) * A_CHUNK, cols] for ci in range(n_chunks)], axis=1)
        m_g = _dot(jnp.where(tri, aws_ref[g], 0.0).astype(BF16), v_g)
        mixed_g.append(jnp.concatenate(
            [m_g[:, ci * A_GROUP_DIM:(ci + 1) * A_GROUP_DIM] for ci in range(n_chunks)], axis=0) + jnp.concatenate(
            [abs_ref[g]] * n_chunks, axis=0))
    mixed = jnp.concatenate(mixed_g, axis=1)
    y_a = a_u * mixed * _silu(p_z)
    g_a = jax.nn.sigmoid(p_ma + bm_ref[:, 0:D_MODEL])
    merged = g_a * _dot(y_a.astype(BF16), w_sq_ref[0, :, 0:D_MODEL])

    for i, (qs, k, b, _) in enumerate(head_args):
        scores_ref[i] = _scores_single_reference(qs, k, b)
    span = jnp.zeros((1, B_KEY_DIM), F32)
    for b in b_chunks:
        b_mid = b[cc // 2 - 1:cc // 2, :]
        span = jnp.maximum(span, jnp.maximum(-b_mid, b_mid - b[cc - 1:cc, :]))
    wide_span = jnp.max(span) >= FAST_SPAN_LOG2

    @pl.when(wide_span)
    def _():
        for i, (qs, k, b, _) in enumerate(head_args):
            scores_ref[i] = _scores_hierarchical(qs, k, b)

    p_bz = proj(OFF_BZ, B_VAL_DIM)
    p_mb = proj(OFF_MB, D_MODEL)
    for ci in range(gla_chunks):
        rs = slice(ci * cc, (ci + 1) * cc)
        heads = []
        for i in range(ci * B_HEADS, (ci + 1) * B_HEADS):
            o_h = o_inter[i] + _dot(scores_ref[i], head_args[i][3])
            heads.append(_rms(o_h) * bng_ref[...])
        o_b = jnp.concatenate(heads, axis=1) * _silu(p_bz[rs])
        y_b = _dot(o_b.astype(BF16), w_sq_ref[1, :, 0:D_MODEL])
        g_b = jax.nn.sigmoid(p_mb[rs] + bm_ref[:, D_MODEL:2 * D_MODEL])
        merged_c = merged[rs] + g_b * y_b
        out = _dot(merged_c.astype(BF16), w_sq_ref[2, :, 0:D_MODEL])
        o_ref[0, rs, :] = _rms(x[rs] + gate * out) * fg_ref[...]


def _const_spec(shape):
    nd = len(shape)
    return pl.BlockSpec(shape, lambda b, j: (0,) * nd, pipeline_mode=pl.Buffered(1))


def kernel(x, c, w_ada, b_ada, norm_g, w_in, a_ln_g, a_ln_b, a_w_s, a_b_s, b_w_gate2, b_b_gate, b_norm_g, w_proj_a, w_proj_b, b_merge, w_out, final_g):
    bsz, seq, d = x.shape
    assert d == D_MODEL and seq % TILE_M == 0 and w_ada.shape[0] == 1 and bsz <= SUBLANES
    l = 0

    row2 = lambda v: v.reshape(1, -1)
    in_hbm = {2, 5, 13, 14, 16}
    operands = (
        x, c, w_ada, b_ada, row2(norm_g[l]), jnp.swapaxes(w_in, 1, 2), row2(a_ln_g[l]), row2(a_ln_b[l]),
        a_w_s[l], a_b_s[l], b_w_gate2[l], row2(b_b_gate[l]), row2(b_norm_g[l]), w_proj_a, w_proj_b,
        row2(b_merge[l]), w_out, row2(final_g),
    )
    in_specs = [pl.BlockSpec((1, TILE_M, d), lambda b, j: (b, j, 0))] + [
        pl.BlockSpec(memory_space=pl.ANY) if i in in_hbm else _const_spec(op.shape)
        for i, op in enumerate(operands) if i >= 1]

    return pl.pallas_call(
        _block_kernel,
        grid=(bsz, seq // TILE_M),
        in_specs=in_specs,
        out_specs=pl.BlockSpec((1, TILE_M, d), lambda b, j: (b, j, 0)),
        out_shape=jax.ShapeDtypeStruct((bsz, seq, d), x.dtype),
        scratch_shapes=[pltpu.VMEM((B_HEADS, B_HEAD_K, B_HEAD_V), F32),
                        pltpu.VMEM((TILE_M // GLA_CHUNK * B_HEADS, GLA_CHUNK, GLA_CHUNK), BF16),
                        pltpu.VMEM((d, W_IN_RESIDENT), BF16),
                        pltpu.VMEM((3, d, W_SQ_RESIDENT), BF16),
                        pltpu.VMEM((SUBLANES, 3 * d), F32),
                        pltpu.VMEM((A_GROUPS, A_CHUNK, A_GROUP_DIM), F32),
                        pltpu.VMEM((STAGE_SLOTS, STAGE_ROWS, D_MODEL), F32),
                        pltpu.SemaphoreType.DMA((STAGE_SLOTS,))],
        compiler_params=pltpu.CompilerParams(
            dimension_semantics=("arbitrary", "arbitrary"),
            vmem_limit_bytes=VMEM_LIMIT_BYTES),
        name="hybrid_block",
    )(*operands)
```

```python
import math

import jax
import jax.numpy as jnp
from jax import lax
from jax.experimental import pallas as pl
from jax.experimental.pallas import tpu as pltpu

D_MODEL = 1024
A_GROUPS = 8
A_CHUNK = 128
A_GROUP_DIM = D_MODEL // A_GROUPS
B_HEADS = 4
B_KEY_DIM = D_MODEL // 2
B_VAL_DIM = D_MODEL
B_HEAD_K = B_KEY_DIM // B_HEADS
B_HEAD_V = B_VAL_DIM // B_HEADS
B_GATE_RANK = 16
B_GATE_NORMALIZER = 16.0
EPS = 1e-6

LANES = 128
SUBLANES = 8
GLA_CHUNK = 256
TILE_M = 512
VMEM_LIMIT_BYTES = 60 * 1024 * 1024

OFF_U, OFF_V, OFF_Z = 0, 1024, 2048
OFF_Q, OFF_K = 3072, 3584
OFF_BV, OFF_BZ = 4096, 5120
W_MAIN = 6 * D_MODEL
IN_WIDTH = W_MAIN + B_GATE_RANK + 2 * D_MODEL
OFF_G1 = W_MAIN
OFF_MA, OFF_MB = W_MAIN + LANES, W_MAIN + LANES + D_MODEL
W_IN_RESIDENT = OFF_MB + D_MODEL
W_SQ_RESIDENT = D_MODEL + LANES
STAGE_ROWS = 128
STAGE_SLOTS = 8

F32 = jnp.float32
BF16 = jnp.bfloat16
LOG2E = 1.4426950408889634
FAST_SPAN_LOG2 = 80.0


def _dot(a, b):
    return jnp.dot(a, b, preferred_element_type=F32)


def _dot_nt(a, b):
    return lax.dot_general(a, b, (((1,), (1,)), ((), ())), preferred_element_type=F32)


def _dot_tn(a, b):
    return lax.dot_general(a, b, (((0,), (0,)), ((), ())), preferred_element_type=F32)


def _split_bf16(a):
    hi = a.astype(BF16)
    lo = (a - hi.astype(F32)).astype(BF16)
    return hi, lo


def _gelu(a):
    return 0.5 * a * (1.0 + lax.erf(a * (1.0 / math.sqrt(2.0))))


def _silu(a):
    return a * jax.nn.sigmoid(a)


def _rms(a):
    return a * lax.rsqrt(jnp.mean(a * a, axis=-1, keepdims=True) + EPS)


def _bcast_row_in_blocks(a, blk, row):
    m, n = a.shape
    a3 = a.reshape(m // blk, blk, n)
    return jnp.broadcast_to(a3[:, row:row + 1, :], (m // blk, blk, n)).reshape(m, n)


def _scores_single_reference(qs, k, b):
    c = qs.shape[0]
    ref_b = b[c // 2 - 1:c // 2, :]
    s = _dot_nt((qs * jnp.exp2(b - ref_b)).astype(BF16), (k * jnp.exp2(ref_b - b)).astype(BF16))
    causal = (lax.broadcasted_iota(jnp.int32, (c, c), 0) >= lax.broadcasted_iota(jnp.int32, (c, c), 1))
    return jnp.where(causal, s, 0.0).astype(BF16)


def _scores_hierarchical(qs, k, b):
    c = qs.shape[0]
    row = lax.broadcasted_iota(jnp.int32, (c, 1), 0)
    col = lax.broadcasted_iota(jnp.int32, (1, c), 1)
    scores = jnp.zeros((c, c), F32)
    m = SUBLANES
    while m < c:
        ref_b = _bcast_row_in_blocks(b, 2 * m, m - 1)
        e = jnp.exp2(-jnp.abs(b - ref_b))
        siblings = ((row // m) == (col // m) + 1) & ((col // m) % 2 == 0)
        scores = jnp.where(siblings, _dot_nt((qs * e).astype(BF16), (k * e).astype(BF16)), scores)
        m *= 2
    row_in = row % SUBLANES
    lane = lax.broadcasted_iota(jnp.int32, (1, LANES), 1)
    diag = jnp.zeros((c, LANES), F32)
    for j in range(SUBLANES):
        b_j = _bcast_row_in_blocks(b, SUBLANES, j)
        k_j = _bcast_row_in_blocks(k, SUBLANES, j)
        s_j = jnp.sum(qs * k_j * jnp.exp2(jnp.minimum(b - b_j, 0.0)), axis=-1, keepdims=True)
        diag = jnp.where((lane == j) & (row_in >= j), s_j, diag)
    expand = (lax.broadcasted_iota(jnp.int32, (LANES, c), 0)
              == lax.broadcasted_iota(jnp.int32, (LANES, c), 1) % SUBLANES).astype(BF16)
    diag_full = _dot(diag.astype(BF16), expand)
    same_block = (row // SUBLANES) == (col // SUBLANES)
    return jnp.where(same_block, diag_full.astype(BF16), scores.astype(BF16))


def _gla_inter_chunk(qs, k, v_b, b, state_ref, h):
    c = qs.shape[0]
    state = state_ref[h]
    o = _dot((qs * jnp.exp2(b)).astype(BF16), state.astype(BF16))
    b_last = b[c - 1:c, :]
    k_dec = (k * jnp.exp2(b_last - b)).astype(BF16)
    decay_col = jnp.transpose(jnp.broadcast_to(jnp.exp2(b_last), (LANES, B_HEAD_K)))[:, 0:1]
    state_ref[h] = decay_col * state + _dot_tn(k_dec, v_b)
    return o


def _stage_weights(w_in_t_hbm, sq_hbm, w_ada_hbm, c_ref, b_ada_ref, w_in_s, w_sq_s, mod_s, stage_ref, sem_ref):
    tasks = []

    bsz = c_ref.shape[0]
    act = _silu(c_ref[...])
    act = jnp.concatenate([act, jnp.zeros((SUBLANES - bsz, D_MODEL), F32)], axis=0)
    mod_s[...] = jnp.broadcast_to(b_ada_ref[...], mod_s.shape)
    for c0 in range(0, 3 * D_MODEL, D_MODEL):
        for r0 in range(0, D_MODEL, STAGE_ROWS):
            def consume(slot, c0=c0, r0=r0):
                a_hi, a_lo = _split_bf16(act[:, r0:r0 + STAGE_ROWS])
                w_hi, w_lo = _split_bf16(stage_ref[slot])
                mod_s[:, c0:c0 + D_MODEL] += _dot(a_hi, w_hi) + (_dot(a_lo, w_hi) + _dot(a_hi, w_lo))
            tasks.append((w_ada_hbm.at[0, pl.ds(r0, STAGE_ROWS), pl.ds(c0, D_MODEL)], STAGE_ROWS, consume))

    def in_task(src_row0, rows, dst_col0, valid):
        def consume(slot):
            blk = jnp.transpose(stage_ref[slot, 0:rows, :])
            if valid < rows:
                blk = jnp.where(lax.broadcasted_iota(jnp.int32, blk.shape, 1) < valid, blk, 0.0)
            w_in_s[:, dst_col0:dst_col0 + rows] = blk.astype(BF16)
        tasks.append((w_in_t_hbm.at[0, pl.ds(src_row0, rows), :], rows, consume))

    for c0 in range(0, W_MAIN, STAGE_ROWS):
        in_task(c0, STAGE_ROWS, c0, STAGE_ROWS)
    in_task(W_MAIN, LANES, OFF_G1, B_GATE_RANK)
    for c0 in range(0, 2 * D_MODEL, STAGE_ROWS):
        in_task(W_MAIN + B_GATE_RANK + c0, STAGE_ROWS, OFF_MA + c0, STAGE_ROWS)
    for m, w_hbm in enumerate(sq_hbm):
        for r0 in range(0, D_MODEL, STAGE_ROWS):
            def consume(slot, m=m, r0=r0):
                w_sq_s[m, r0:r0 + STAGE_ROWS, 0:D_MODEL] = stage_ref[slot].astype(BF16)
            tasks.append((w_hbm.at[0, pl.ds(r0, STAGE_ROWS), :], STAGE_ROWS, consume))

    copies = [pltpu.make_async_copy(src, stage_ref.at[i % STAGE_SLOTS, pl.ds(0, rows), :],
                                    sem_ref.at[i % STAGE_SLOTS])
              for i, (src, rows, _) in enumerate(tasks)]
    ahead = STAGE_SLOTS - 1
    for cp in copies[:ahead]:
        cp.start()
    for i, (_, _, consume) in enumerate(tasks):
        if i + ahead < len(tasks):
            copies[i + ahead].start()
        copies[i].wait()
        consume(i % STAGE_SLOTS)


def _block_kernel(x_ref, c_ref, w_ada_hbm, b_ada_ref, ng_ref, w_in_hbm, alng_ref, alnb_ref, aws_ref, abs_row_ref,
                  wg2_ref, bbg_ref, bng_ref, wpa_hbm, wpb_hbm, bm_ref, wout_hbm, fg_ref,
                  o_ref, state_ref, scores_ref, w_in_ref, w_sq_ref, mod_ref, abs_ref, stage_ref, sem_ref):
    @pl.when((pl.program_id(0) == 0) & (pl.program_id(1) == 0))
    def _():
        _stage_weights(w_in_hbm, (wpa_hbm, wpb_hbm, wout_hbm), w_ada_hbm, c_ref, b_ada_ref,
                       w_in_ref, w_sq_ref, mod_ref, stage_ref, sem_ref)
        for g in range(A_GROUPS):
            abs_ref[g] = jnp.transpose(jnp.broadcast_to(abs_row_ref[g:g + 1, :], (A_GROUP_DIM, A_CHUNK)))

    @pl.when(pl.program_id(1) == 0)
    def _():
        state_ref[...] = jnp.zeros_like(state_ref)

    tm = x_ref.shape[1]
    mod = mod_ref[pl.ds(pl.program_id(0), 1), :]
    shift = mod[:, 0:D_MODEL]
    scale = mod[:, D_MODEL:2 * D_MODEL]
    gate = mod[:, 2 * D_MODEL:3 * D_MODEL]

    cc = GLA_CHUNK
    gla_chunks = tm // cc
    tril_b = (lax.broadcasted_iota(jnp.int32, (cc, cc), 0)
              >= lax.broadcasted_iota(jnp.int32, (cc, cc), 1)).astype(BF16)
    tri = (lax.broadcasted_iota(jnp.int32, (A_CHUNK, A_CHUNK), 0)
           >= lax.broadcasted_iota(jnp.int32, (A_CHUNK, A_CHUNK), 1))
    wg2 = jnp.concatenate([wg2_ref[...], jnp.zeros((LANES - B_GATE_RANK, B_KEY_DIM), F32)],
                          axis=0).astype(BF16)
    n_sub = cc // A_CHUNK

    def before_scores(ci):
        x_c = x_ref[0, ci * cc:(ci + 1) * cc, :]
        hn_b = (_rms(x_c) * ng_ref[...] * (1.0 + scale) + shift).astype(BF16)

        def proj(lo, width):
            return _dot(hn_b, w_in_ref[:, lo:lo + width])

        g1 = proj(OFF_G1, LANES)
        q_c = proj(OFF_Q, B_KEY_DIM) * (B_HEAD_K ** -0.5)
        k_c = proj(OFF_K, B_KEY_DIM)
        gk = _dot(g1.astype(BF16), wg2) + bbg_ref[...]
        p_bv = proj(OFF_BV, B_VAL_DIM)
        log2_a = (jnp.minimum(gk, 0.0) - jnp.log1p(jnp.exp(-jnp.abs(gk)))) * (LOG2E / B_GATE_NORMALIZER)
        la_hi, la_lo = _split_bf16(log2_a)
        p_u = proj(OFF_U, D_MODEL)
        b_c = _dot(tril_b, la_hi) + _dot(tril_b, la_lo)
        v_c = p_bv.astype(BF16)
        heads = [(q_c[:, h * B_HEAD_K:(h + 1) * B_HEAD_K], k_c[:, h * B_HEAD_K:(h + 1) * B_HEAD_K],
                  b_c[:, h * B_HEAD_K:(h + 1) * B_HEAD_K], v_c[:, h * B_HEAD_V:(h + 1) * B_HEAD_V])
                 for h in range(B_HEADS)]

        p_v = proj(OFF_V, D_MODEL)
        o_inter = [_gla_inter_chunk(qs, k, v_b, b, state_ref, h) for h, (qs, k, b, v_b) in enumerate(heads)]

        p_z = proj(OFF_Z, D_MODEL)
        a_u = _gelu(p_u)
        a_v = _gelu(p_v)
        p_ma = proj(OFF_MA, D_MODEL)
        a_vc = a_v - jnp.mean(a_v, axis=-1, keepdims=True)
        a_vn = (a_vc * lax.rsqrt(jnp.mean(a_vc * a_vc, axis=-1, keepdims=True) + EPS)
                * alng_ref[...] + alnb_ref[...]).astype(BF16)
        mixed_g = []
        for g in range(A_GROUPS):
            cols = slice(g * A_GROUP_DIM, (g + 1) * A_GROUP_DIM)
            v_g = jnp.concatenate([a_vn[si * A_CHUNK:(si + 1) * A_CHUNK, cols] for si in range(n_sub)], axis=1)
            m_g = _dot(jnp.where(tri, aws_ref[g], 0.0).astype(BF16), v_g)
            mixed_g.append(jnp.concatenate(
                [m_g[:, si * A_GROUP_DIM:(si + 1) * A_GROUP_DIM] for si in range(n_sub)], axis=0) + jnp.concatenate(
                [abs_ref[g]] * n_sub, axis=0))
        y_a = a_u * jnp.concatenate(mixed_g, axis=1) * _silu(p_z)
        g_a = jax.nn.sigmoid(p_ma + bm_ref[:, 0:D_MODEL])
        merged_a = g_a * _dot(y_a.astype(BF16), w_sq_ref[0, :, 0:D_MODEL])

        for h, (qs, k, b, _) in enumerate(heads):
            scores_ref[ci * B_HEADS + h] = _scores_single_reference(qs, k, b)
        b_mid = b_c[cc // 2 - 1:cc // 2, :]
        span = jnp.maximum(-b_mid, b_mid - b_c[cc - 1:cc, :])
        return dict(x=x_c, hn=hn_b, heads=heads, o_inter=o_inter, merged_a=merged_a, span=span)

    blocks = [before_scores(ci) for ci in range(gla_chunks)]

    span = blocks[0]["span"]
    for blk in blocks[1:]:
        span = jnp.maximum(span, blk["span"])
    wide_span = jnp.max(span) >= FAST_SPAN_LOG2

    @pl.when(wide_span)
    def _():
        for ci, blk in enumerate(blocks):
            for h, (qs, k, b, _) in enumerate(blk["heads"]):
                scores_ref[ci * B_HEADS + h] = _scores_hierarchical(qs, k, b)

    for ci, blk in enumerate(blocks):
        p_bz = _dot(blk["hn"], w_in_ref[:, OFF_BZ:OFF_BZ + B_VAL_DIM])
        p_mb = _dot(blk["hn"], w_in_ref[:, OFF_MB:OFF_MB + D_MODEL])
        o_heads = []
        for h, (_, _, _, v_b) in enumerate(blk["heads"]):
            o_h = blk["o_inter"][h] + _dot(scores_ref[ci * B_HEADS + h], v_b)
            o_heads.append(_rms(o_h) * bng_ref[...])
        o_b = jnp.concatenate(o_heads, axis=1) * _silu(p_bz)
        y_b = _dot(o_b.astype(BF16), w_sq_ref[1, :, 0:D_MODEL])
        g_b = jax.nn.sigmoid(p_mb + bm_ref[:, D_MODEL:2 * D_MODEL])
        merged = blk["merged_a"] + g_b * y_b
        out = _dot(merged.astype(BF16), w_sq_ref[2, :, 0:D_MODEL])
        o_ref[0, ci * cc:(ci + 1) * cc, :] = _rms(blk["x"] + gate * out) * fg_ref[...]


def _const_spec(shape):
    nd = len(shape)
    return pl.BlockSpec(shape, lambda b, j: (0,) * nd, pipeline_mode=pl.Buffered(1))


def kernel(x, c, w_ada, b_ada, norm_g, w_in, a_ln_g, a_ln_b, a_w_s, a_b_s, b_w_gate2, b_b_gate, b_norm_g, w_proj_a, w_proj_b, b_merge, w_out, final_g):
    bsz, seq, d = x.shape
    assert d == D_MODEL and seq % TILE_M == 0 and w_ada.shape[0] == 1 and bsz <= SUBLANES
    l = 0

    row2 = lambda v: v.reshape(1, -1)
    in_hbm = {2, 5, 13, 14, 16}
    operands = (
        x, c, w_ada, b_ada, row2(norm_g[l]), jnp.swapaxes(w_in, 1, 2), row2(a_ln_g[l]), row2(a_ln_b[l]),
        a_w_s[l], a_b_s[l], b_w_gate2[l], row2(b_b_gate[l]), row2(b_norm_g[l]), w_proj_a, w_proj_b,
        row2(b_merge[l]), w_out, row2(final_g),
    )
    in_specs = [pl.BlockSpec((1, TILE_M, d), lambda b, j: (b, j, 0))] + [
        pl.BlockSpec(memory_space=pl.ANY) if i in in_hbm else _const_spec(op.shape)
        for i, op in enumerate(operands) if i >= 1]

    return pl.pallas_call(
        _block_kernel,
        grid=(bsz, seq // TILE_M),
        in_specs=in_specs,
        out_specs=pl.BlockSpec((1, TILE_M, d), lambda b, j: (b, j, 0)),
        out_shape=jax.ShapeDtypeStruct((bsz, seq, d), x.dtype),
        scratch_shapes=[pltpu.VMEM((B_HEADS, B_HEAD_K, B_HEAD_V), F32),
                        pltpu.VMEM((TILE_M // GLA_CHUNK * B_HEADS, GLA_CHUNK, GLA_CHUNK), BF16),
                        pltpu.VMEM((d, W_IN_RESIDENT), BF16),
                        pltpu.VMEM((3, d, W_SQ_RESIDENT), BF16),
                        pltpu.VMEM((SUBLANES, 3 * d), F32),
                        pltpu.VMEM((A_GROUPS, A_CHUNK, A_GROUP_DIM), F32),
                        pltpu.VMEM((STAGE_SLOTS, STAGE_ROWS, D_MODEL), F32),
                        pltpu.SemaphoreType.DMA((STAGE_SLOTS,))],
        compiler_params=pltpu.CompilerParams(
            dimension_semantics=("arbitrary", "arbitrary"),
            vmem_limit_bytes=VMEM_LIMIT_BYTES),
        name="hybrid_block",
    )(*operands)
```

```python
import math

import jax
import jax.numpy as jnp
from jax import lax
from jax.experimental import pallas as pl
from jax.experimental.pallas import tpu as pltpu

D_MODEL = 1024
A_GROUPS = 8
A_CHUNK = 128
A_GROUP_DIM = D_MODEL // A_GROUPS
B_HEADS = 4
B_KEY_DIM = D_MODEL // 2
B_VAL_DIM = D_MODEL
B_HEAD_K = B_KEY_DIM // B_HEADS
B_HEAD_V = B_VAL_DIM // B_HEADS
B_GATE_RANK = 16
B_GATE_NORMALIZER = 16.0
EPS = 1e-6

LANES = 128
SUBLANES = 8
GLA_CHUNK = 256
TILE_M = 512
VMEM_LIMIT_BYTES = 60 * 1024 * 1024

OFF_U, OFF_V, OFF_Z = 0, 1024, 2048
OFF_Q, OFF_K = 3072, 3584
OFF_BV, OFF_BZ = 4096, 5120
W_MAIN = 6 * D_MODEL
IN_WIDTH = W_MAIN + B_GATE_RANK + 2 * D_MODEL
OFF_G1 = W_MAIN
OFF_MA, OFF_MB = W_MAIN + LANES, W_MAIN + LANES + D_MODEL
W_IN_RESIDENT = OFF_MB + D_MODEL
W_SQ_RESIDENT = D_MODEL + LANES
STAGE_ROWS = 128
STAGE_SLOTS = 8

F32 = jnp.float32
BF16 = jnp.bfloat16
LOG2E = 1.4426950408889634
FAST_SPAN_LOG2 = 80.0


def _dot(a, b):
    return jnp.dot(a, b, preferred_element_type=F32)


def _dot_nt(a, b):
    return lax.dot_general(a, b, (((1,), (1,)), ((), ())), preferred_element_type=F32)


def _dot_tn(a, b):
    return lax.dot_general(a, b, (((0,), (0,)), ((), ())), preferred_element_type=F32)


def _split_bf16(a):
    hi = a.astype(BF16)
    lo = (a - hi.astype(F32)).astype(BF16)
    return hi, lo


def _gelu(a):
    return 0.5 * a * (1.0 + lax.erf(a * (1.0 / math.sqrt(2.0))))


def _silu(a):
    return a * jax.nn.sigmoid(a)


def _rms(a):
    return a * lax.rsqrt(jnp.mean(a * a, axis=-1, keepdims=True) + EPS)


def _bcast_row_in_blocks(a, blk, row):
    m, n = a.shape
    a3 = a.reshape(m // blk, blk, n)
    return jnp.broadcast_to(a3[:, row:row + 1, :], (m // blk, blk, n)).reshape(m, n)


def _scores_single_reference(qs, k, b):
    c = qs.shape[0]
    ref_b = b[c // 2 - 1:c // 2, :]
    s = _dot_nt((qs * jnp.exp2(b - ref_b)).astype(BF16), (k * jnp.exp2(ref_b - b)).astype(BF16))
    causal = (lax.broadcasted_iota(jnp.int32, (c, c), 0) >= lax.broadcasted_iota(jnp.int32, (c, c), 1))
    return jnp.where(causal, s.astype(BF16), jnp.zeros((), BF16))


def _scores_hierarchical(qs, k, b):
    c = qs.shape[0]
    row = lax.broadcasted_iota(jnp.int32, (c, 1), 0)
    col = lax.broadcasted_iota(jnp.int32, (1, c), 1)
    scores = jnp.zeros((c, c), F32)
    m = SUBLANES
    while m < c:
        ref_b = _bcast_row_in_blocks(b, 2 * m, m - 1)
        e = jnp.exp2(-jnp.abs(b - ref_b))
        siblings = ((row // m) == (col // m) + 1) & ((col // m) % 2 == 0)
        scores = jnp.where(siblings, _dot_nt((qs * e).astype(BF16), (k * e).astype(BF16)), scores)
        m *= 2
    row_in = row % SUBLANES
    lane = lax.broadcasted_iota(jnp.int32, (1, LANES), 1)
    diag = jnp.zeros((c, LANES), F32)
    for j in range(SUBLANES):
        b_j = _bcast_row_in_blocks(b, SUBLANES, j)
        k_j = _bcast_row_in_blocks(k, SUBLANES, j)
        s_j = jnp.sum(qs * k_j * jnp.exp2(jnp.minimum(b - b_j, 0.0)), axis=-1, keepdims=True)
        diag = jnp.where((lane == j) & (row_in >= j), s_j, diag)
    expand = (lax.broadcasted_iota(jnp.int32, (LANES, c), 0)
              == lax.broadcasted_iota(jnp.int32, (LANES, c), 1) % SUBLANES).astype(BF16)
    diag_full = _dot(diag.astype(BF16), expand)
    same_block = (row // SUBLANES) == (col // SUBLANES)
    return jnp.where(same_block, diag_full.astype(BF16), scores.astype(BF16))


def _gla_inter_chunk(qs, k, v_b, b, state_ref, h):
    c = qs.shape[0]
    state = state_ref[h]
    o = _dot((qs * jnp.exp2(b)).astype(BF16), state.astype(BF16))
    b_last = b[c - 1:c, :]
    k_dec = (k * jnp.exp2(b_last - b)).astype(BF16)
    decay_col = jnp.transpose(jnp.broadcast_to(jnp.exp2(b_last), (LANES, B_HEAD_K)))[:, 0:1]
    state_ref[h] = decay_col * state + _dot_tn(k_dec, v_b)
    return o


def _stage_weights(w_in_t_hbm, sq_hbm, w_ada_hbm, c_ref, b_ada_ref, w_in_s, w_sq_s, mod_s, stage_ref, sem_ref):
    tasks = []

    bsz = c_ref.shape[0]
    act = _silu(c_ref[...])
    act = jnp.concatenate([act, jnp.zeros((SUBLANES - bsz, D_MODEL), F32)], axis=0)
    mod_s[...] = jnp.broadcast_to(b_ada_ref[...], mod_s.shape)
    for c0 in range(0, 3 * D_MODEL, D_MODEL):
        for r0 in range(0, D_MODEL, STAGE_ROWS):
            def consume(slot, c0=c0, r0=r0):
                a_hi, a_lo = _split_bf16(act[:, r0:r0 + STAGE_ROWS])
                w_hi, w_lo = _split_bf16(stage_ref[slot])
                mod_s[:, c0:c0 + D_MODEL] += _dot(a_hi, w_hi) + (_dot(a_lo, w_hi) + _dot(a_hi, w_lo))
            tasks.append((w_ada_hbm.at[0, pl.ds(r0, STAGE_ROWS), pl.ds(c0, D_MODEL)], STAGE_ROWS, consume))

    def in_task(src_row0, rows, dst_col0, valid):
        def consume(slot):
            blk = jnp.transpose(stage_ref[slot, 0:rows, :])
            if valid < rows:
                blk = jnp.where(lax.broadcasted_iota(jnp.int32, blk.shape, 1) < valid, blk, 0.0)
            w_in_s[:, dst_col0:dst_col0 + rows] = blk.astype(BF16)
        tasks.append((w_in_t_hbm.at[0, pl.ds(src_row0, rows), :], rows, consume))

    for c0 in range(0, W_MAIN, STAGE_ROWS):
        in_task(c0, STAGE_ROWS, c0, STAGE_ROWS)
    in_task(W_MAIN, LANES, OFF_G1, B_GATE_RANK)
    for c0 in range(0, 2 * D_MODEL, STAGE_ROWS):
        in_task(W_MAIN + B_GATE_RANK + c0, STAGE_ROWS, OFF_MA + c0, STAGE_ROWS)
    for m, w_hbm in enumerate(sq_hbm):
        for r0 in range(0, D_MODEL, STAGE_ROWS):
            def consume(slot, m=m, r0=r0):
                w_sq_s[m, r0:r0 + STAGE_ROWS, 0:D_MODEL] = stage_ref[slot].astype(BF16)
            tasks.append((w_hbm.at[0, pl.ds(r0, STAGE_ROWS), :], STAGE_ROWS, consume))

    copies = [pltpu.make_async_copy(src, stage_ref.at[i % STAGE_SLOTS, pl.ds(0, rows), :],
                                    sem_ref.at[i % STAGE_SLOTS])
              for i, (src, rows, _) in enumerate(tasks)]
    ahead = STAGE_SLOTS - 1
    for cp in copies[:ahead]:
        cp.start()
    for i, (_, _, consume) in enumerate(tasks):
        if i + ahead < len(tasks):
            copies[i + ahead].start()
        copies[i].wait()
        consume(i % STAGE_SLOTS)


def _block_kernel(x_ref, c_ref, w_ada_hbm, b_ada_ref, ng_ref, w_in_hbm, alng_ref, alnb_ref, aws_ref, abs_row_ref,
                  wg2_ref, bbg_ref, bng_ref, wpa_hbm, wpb_hbm, bm_ref, wout_hbm, fg_ref,
                  o_ref, state_ref, scores_ref, w_in_ref, w_sq_ref, mod_ref, abs_ref, stage_ref, sem_ref):
    @pl.when((pl.program_id(0) == 0) & (pl.program_id(1) == 0))
    def _():
        _stage_weights(w_in_hbm, (wpa_hbm, wpb_hbm, wout_hbm), w_ada_hbm, c_ref, b_ada_ref,
                       w_in_ref, w_sq_ref, mod_ref, stage_ref, sem_ref)
        for g in range(A_GROUPS):
            abs_ref[g] = jnp.transpose(jnp.broadcast_to(abs_row_ref[g:g + 1, :], (A_GROUP_DIM, A_CHUNK)))

    @pl.when(pl.program_id(1) == 0)
    def _():
        state_ref[...] = jnp.zeros_like(state_ref)

    tm = x_ref.shape[1]
    x = x_ref[0]
    mod = mod_ref[pl.ds(pl.program_id(0), 1), :]
    shift = mod[:, 0:D_MODEL]
    scale = mod[:, D_MODEL:2 * D_MODEL]
    gate = mod[:, 2 * D_MODEL:3 * D_MODEL]

    hn_b = (_rms(x) * (ng_ref[...] * (1.0 + scale)) + shift).astype(BF16)

    def proj(lo, width, w_ref=w_in_ref):
        return _dot(hn_b, w_ref[:, lo:lo + width])

    g1 = proj(OFF_G1, LANES)
    q_all = proj(OFF_Q, B_KEY_DIM) * (B_HEAD_K ** -0.5)
    k_all = proj(OFF_K, B_KEY_DIM)
    wg2 = jnp.concatenate([wg2_ref[...], jnp.zeros((LANES - B_GATE_RANK, B_KEY_DIM), F32)], axis=0)
    gk = _dot(g1.astype(BF16), wg2.astype(BF16)) + bbg_ref[...]
    p_bv = proj(OFF_BV, B_VAL_DIM)
    log2_a = (jnp.minimum(gk, 0.0) - jnp.log1p(jnp.exp(-jnp.abs(gk)))) * (LOG2E / B_GATE_NORMALIZER)
    cc = GLA_CHUNK
    gla_chunks = tm // cc
    tril_b = (lax.broadcasted_iota(jnp.int32, (cc, cc), 0)
              >= lax.broadcasted_iota(jnp.int32, (cc, cc), 1)).astype(BF16)
    la_hi, la_lo = _split_bf16(log2_a)
    p_u = proj(OFF_U, D_MODEL)
    b_chunks = [_dot(tril_b, la_hi[ci * cc:(ci + 1) * cc]) + _dot(tril_b, la_lo[ci * cc:(ci + 1) * cc])
                for ci in range(gla_chunks)]
    v_all = p_bv.astype(BF16)

    head_args = []
    for ci in range(gla_chunks):
        rs = slice(ci * cc, (ci + 1) * cc)
        for h in range(B_HEADS):
            ks = slice(h * B_HEAD_K, (h + 1) * B_HEAD_K)
            head_args.append((q_all[rs, ks], k_all[rs, ks], b_chunks[ci][:, ks],
                              v_all[rs, h * B_HEAD_V:(h + 1) * B_HEAD_V]))

    p_v = proj(OFF_V, D_MODEL)
    o_inter = [_gla_inter_chunk(qs, k, v_b, b, state_ref, i % B_HEADS)
               for i, (qs, k, b, v_b) in enumerate(head_args)]

    p_z = proj(OFF_Z, D_MODEL)
    a_u = _gelu(p_u)
    a_v = _gelu(p_v)
    p_ma = proj(OFF_MA, D_MODEL)
    a_vc = a_v - jnp.mean(a_v, axis=-1, keepdims=True)
    a_vn = (a_vc * lax.rsqrt(jnp.mean(a_vc * a_vc, axis=-1, keepdims=True) + EPS)
            * alng_ref[...] + alnb_ref[...]).astype(BF16)

    tri = (lax.broadcasted_iota(jnp.int32, (A_CHUNK, A_CHUNK), 0)
           >= lax.broadcasted_iota(jnp.int32, (A_CHUNK, A_CHUNK), 1))
    n_chunks = tm // A_CHUNK
    mixed_g = []
    for g in range(A_GROUPS):
        cols = slice(g * A_GROUP_DIM, (g + 1) * A_GROUP_DIM)
        v_g = jnp.concatenate([a_vn[ci * A_CHUNK:(ci + 1) * A_CHUNK, cols] for ci in range(n_chunks)], axis=1)
        m_g = _dot(jnp.where(tri, aws_ref[g], 0.0).astype(BF16), v_g)
        mixed_g.append(jnp.concatenate(
            [m_g[:, ci * A_GROUP_DIM:(ci + 1) * A_GROUP_DIM] for ci in range(n_chunks)], axis=0) + jnp.concatenate(
            [abs_ref[g]] * n_chunks, axis=0))
    mixed = jnp.concatenate(mixed_g, axis=1)
    y_a = a_u * mixed * _silu(p_z)
    g_a = jax.nn.sigmoid(p_ma + bm_ref[:, 0:D_MODEL])
    merged = g_a * _dot(y_a.astype(BF16), w_sq_ref[0, :, 0:D_MODEL])

    for i, (qs, k, b, _) in enumerate(head_args):
        scores_ref[i] = _scores_single_reference(qs, k, b)
    span = jnp.zeros((1, B_KEY_DIM), F32)
    for b in b_chunks:
        b_mid = b[cc // 2 - 1:cc // 2, :]
        span = jnp.maximum(span, jnp.maximum(-b_mid, b_mid - b[cc - 1:cc, :]))
    wide_span = jnp.max(span) >= FAST_SPAN_LOG2

    @pl.when(wide_span)
    def _():
        for i, (qs, k, b, _) in enumerate(head_args):
            scores_ref[i] = _scores_hierarchical(qs, k, b)

    p_bz = proj(OFF_BZ, B_VAL_DIM)
    p_mb = proj(OFF_MB, D_MODEL)
    for ci in range(gla_chunks):
        rs = slice(ci * cc, (ci + 1) * cc)
        heads = []
        for i in range(ci * B_HEADS, (ci + 1) * B_HEADS):
            o_h = o_inter[i] + _dot(scores_ref[i], head_args[i][3])
            heads.append(_rms(o_h) * bng_ref[...])
        o_b = jnp.concatenate(heads, axis=1) * _silu(p_bz[rs])
        y_b = _dot(o_b.astype(BF16), w_sq_ref[1, :, 0:D_MODEL])
        g_b = jax.nn.sigmoid(p_mb[rs] + bm_ref[:, D_MODEL:2 * D_MODEL])
        merged_c = merged[rs] + g_b * y_b
        out = _dot(merged_c.astype(BF16), w_sq_ref[2, :, 0:D_MODEL])
        o_ref[0, rs, :] = _rms(x[rs] + gate * out) * fg_ref[...]


def _const_spec(shape):
    nd = len(shape)
    return pl.BlockSpec(shape, lambda b, j: (0,) * nd, pipeline_mode=pl.Buffered(1))


def kernel(x, c, w_ada, b_ada, norm_g, w_in, a_ln_g, a_ln_b, a_w_s, a_b_s, b_w_gate2, b_b_gate, b_norm_g, w_proj_a, w_proj_b, b_merge, w_out, final_g):
    bsz, seq, d = x.shape
    assert d == D_MODEL and seq % TILE_M == 0 and w_ada.shape[0] == 1 and bsz <= SUBLANES
    l = 0

    row2 = lambda v: v.reshape(1, -1)
    in_hbm = {2, 5, 13, 14, 16}
    operands = (
        x, c, w_ada, b_ada, row2(norm_g[l]), jnp.swapaxes(w_in, 1, 2), row2(a_ln_g[l]), row2(a_ln_b[l]),
        a_w_s[l], a_b_s[l], b_w_gate2[l], row2(b_b_gate[l]), row2(b_norm_g[l]), w_proj_a, w_proj_b,
        row2(b_merge[l]), w_out, row2(final_g),
    )
    in_specs = [pl.BlockSpec((1, TILE_M, d), lambda b, j: (b, j, 0))] + [
        pl.BlockSpec(memory_space=pl.ANY) if i in in_hbm else _const_spec(op.shape)
        for i, op in enumerate(operands) if i >= 1]

    return pl.pallas_call(
        _block_kernel,
        grid=(bsz, seq // TILE_M),
        in_specs=in_specs,
        out_specs=pl.BlockSpec((1, TILE_M, d), lambda b, j: (b, j, 0)),
        out_shape=jax.ShapeDtypeStruct((bsz, seq, d), x.dtype),
        scratch_shapes=[pltpu.VMEM((B_HEADS, B_HEAD_K, B_HEAD_V), F32),
                        pltpu.VMEM((TILE_M // GLA_CHUNK * B_HEADS, GLA_CHUNK, GLA_CHUNK), BF16),
                        pltpu.VMEM((d, W_IN_RESIDENT), BF16),
                        pltpu.VMEM((3, d, W_SQ_RESIDENT), BF16),
                        pltpu.VMEM((SUBLANES, 3 * d), F32),
                        pltpu.VMEM((A_GROUPS, A_CHUNK, A_GROUP_DIM), F32),
                        pltpu.VMEM((STAGE_SLOTS, STAGE_ROWS, D_MODEL), F32),
                        pltpu.SemaphoreType.DMA((STAGE_SLOTS,))],
        compiler_params=pltpu.CompilerParams(
            dimension_semantics=("arbitrary", "arbitrary"),
            vmem_limit_bytes=VMEM_LIMIT_BYTES),
        name="hybrid_block",
    )(*operands)
```

```python
import math

import jax
import jax.numpy as jnp
from jax import lax
from jax.experimental import pallas as pl
from jax.experimental.pallas import tpu as pltpu

D_MODEL = 1024
A_GROUPS = 8
A_CHUNK = 128
A_GROUP_DIM = D_MODEL // A_GROUPS
B_HEADS = 4
B_KEY_DIM = D_MODEL // 2
B_VAL_DIM = D_MODEL
B_HEAD_K = B_KEY_DIM // B_HEADS
B_HEAD_V = B_VAL_DIM // B_HEADS
B_GATE_RANK = 16
B_GATE_NORMALIZER = 16.0
EPS = 1e-6

LANES = 128
SUBLANES = 8
GLA_CHUNK = 256
TILE_M = 512
VMEM_LIMIT_BYTES = 60 * 1024 * 1024

OFF_U, OFF_V, OFF_Z = 0, 1024, 2048
OFF_Q, OFF_K = 3072, 3584
OFF_BV, OFF_BZ = 4096, 5120
W_MAIN = 6 * D_MODEL
IN_WIDTH = W_MAIN + B_GATE_RANK + 2 * D_MODEL
OFF_G1 = W_MAIN
OFF_MA, OFF_MB = W_MAIN + LANES, W_MAIN + LANES + D_MODEL
W_IN_RESIDENT = OFF_MB + D_MODEL
W_SQ_RESIDENT = D_MODEL + LANES
STAGE_ROWS = 128
STAGE_SLOTS = 8

F32 = jnp.float32
BF16 = jnp.bfloat16
LOG2E = 1.4426950408889634
FAST_SPAN_LOG2 = 80.0


def _dot(a, b):
    return jnp.dot(a, b, preferred_element_type=F32)


def _dot_nt(a, b):
    return lax.dot_general(a, b, (((1,), (1,)), ((), ())), preferred_element_type=F32)


def _dot_tn(a, b):
    return lax.dot_general(a, b, (((0,), (0,)), ((), ())), preferred_element_type=F32)


def _split_bf16(a):
    hi = a.astype(BF16)
    lo = (a - hi.astype(F32)).astype(BF16)
    return hi, lo


def _gelu(a):
    return 0.5 * a * (1.0 + lax.erf(a * (1.0 / math.sqrt(2.0))))


def _silu(a):
    return a * jax.nn.sigmoid(a)


def _rms(a):
    return a * lax.rsqrt(jnp.mean(a * a, axis=-1, keepdims=True) + EPS)


def _bcast_row_in_blocks(a, blk, row):
    m, n = a.shape
    a3 = a.reshape(m // blk, blk, n)
    return jnp.broadcast_to(a3[:, row:row + 1, :], (m // blk, blk, n)).reshape(m, n)


def _scores_single_reference(qs, k, b):
    c = qs.shape[0]
    ref_b = b[c // 2 - 1:c // 2, :]
    s = _dot_nt((qs * jnp.exp2(b - ref_b)).astype(BF16), (k * jnp.exp2(ref_b - b)).astype(BF16))
    causal = (lax.broadcasted_iota(jnp.int32, (c, c), 0) >= lax.broadcasted_iota(jnp.int32, (c, c), 1))
    return jnp.where(causal, s.astype(BF16), jnp.zeros((), BF16))


def _scores_hierarchical(qs, k, b):
    c = qs.shape[0]
    row = lax.broadcasted_iota(jnp.int32, (c, 1), 0)
    col = lax.broadcasted_iota(jnp.int32, (1, c), 1)
    scores = jnp.zeros((c, c), F32)
    m = SUBLANES
    while m < c:
        ref_b = _bcast_row_in_blocks(b, 2 * m, m - 1)
        e = jnp.exp2(-jnp.abs(b - ref_b))
        siblings = ((row // m) == (col // m) + 1) & ((col // m) % 2 == 0)
        scores = jnp.where(siblings, _dot_nt((qs * e).astype(BF16), (k * e).astype(BF16)), scores)
        m *= 2
    row_in = row % SUBLANES
    lane = lax.broadcasted_iota(jnp.int32, (1, LANES), 1)
    diag = jnp.zeros((c, LANES), F32)
    for j in range(SUBLANES):
        b_j = _bcast_row_in_blocks(b, SUBLANES, j)
        k_j = _bcast_row_in_blocks(k, SUBLANES, j)
        s_j = jnp.sum(qs * k_j * jnp.exp2(jnp.minimum(b - b_j, 0.0)), axis=-1, keepdims=True)
        diag = jnp.where((lane == j) & (row_in >= j), s_j, diag)
    expand = (lax.broadcasted_iota(jnp.int32, (LANES, c), 0)
              == lax.broadcasted_iota(jnp.int32, (LANES, c), 1) % SUBLANES).astype(BF16)
    diag_full = _dot(diag.astype(BF16), expand)
    same_block = (row // SUBLANES) == (col // SUBLANES)
    return jnp.where(same_block, diag_full.astype(BF16), scores.astype(BF16))


def _gla_inter_chunk(qs, k, v_b, b, state_ref, h):
    c = qs.shape[0]
    state = state_ref[h]
    o = _dot((qs * jnp.exp2(b)).astype(BF16), state.astype(BF16))
    b_last = b[c - 1:c, :]
    k_dec = (k * jnp.exp2(b_last - b)).astype(BF16)
    decay_col = jnp.transpose(jnp.broadcast_to(jnp.exp2(b_last), (LANES, B_HEAD_K)))[:, 0:1]
    state_ref[h] = decay_col * state + _dot_tn(k_dec, v_b)
    return o


def _stage_weights(w_in_t_hbm, sq_hbm, w_ada_hbm, c_ref, b_ada_ref, w_in_s, w_sq_s, mod_s, stage_ref, sem_ref):
    tasks = []

    bsz = c_ref.shape[0]
    act = _silu(c_ref[...])
    act = jnp.concatenate([act, jnp.zeros((SUBLANES - bsz, D_MODEL), F32)], axis=0)
    mod_s[...] = jnp.broadcast_to(b_ada_ref[...], mod_s.shape)
    for c0 in range(0, 3 * D_MODEL, D_MODEL):
        for r0 in range(0, D_MODEL, STAGE_ROWS):
            def consume(slot, c0=c0, r0=r0):
                a_hi, a_lo = _split_bf16(act[:, r0:r0 + STAGE_ROWS])
                w_hi, w_lo = _split_bf16(stage_ref[slot])
                mod_s[:, c0:c0 + D_MODEL] += _dot(a_hi, w_hi) + (_dot(a_lo, w_hi) + _dot(a_hi, w_lo))
            tasks.append((w_ada_hbm.at[0, pl.ds(r0, STAGE_ROWS), pl.ds(c0, D_MODEL)], STAGE_ROWS, consume))

    def in_task(src_row0, rows, dst_col0, valid):
        def consume(slot):
            blk = jnp.transpose(stage_ref[slot, 0:rows, :])
            if valid < rows:
                blk = jnp.where(lax.broadcasted_iota(jnp.int32, blk.shape, 1) < valid, blk, 0.0)
            w_in_s[:, dst_col0:dst_col0 + rows] = blk.astype(BF16)
        tasks.append((w_in_t_hbm.at[0, pl.ds(src_row0, rows), :], rows, consume))

    for c0 in range(0, W_MAIN, STAGE_ROWS):
        in_task(c0, STAGE_ROWS, c0, STAGE_ROWS)
    in_task(W_MAIN, LANES, OFF_G1, B_GATE_RANK)
    for c0 in range(0, 2 * D_MODEL, STAGE_ROWS):
        in_task(W_MAIN + B_GATE_RANK + c0, STAGE_ROWS, OFF_MA + c0, STAGE_ROWS)
    for m, w_hbm in enumerate(sq_hbm):
        for r0 in range(0, D_MODEL, STAGE_ROWS):
            def consume(slot, m=m, r0=r0):
                w_sq_s[m, r0:r0 + STAGE_ROWS, 0:D_MODEL] = stage_ref[slot].astype(BF16)
            tasks.append((w_hbm.at[0, pl.ds(r0, STAGE_ROWS), :], STAGE_ROWS, consume))

    copies = [pltpu.make_async_copy(src, stage_ref.at[i % STAGE_SLOTS, pl.ds(0, rows), :],
                                    sem_ref.at[i % STAGE_SLOTS])
              for i, (src, rows, _) in enumerate(tasks)]
    ahead = STAGE_SLOTS - 1
    for cp in copies[:ahead]:
        cp.start()
    for i, (_, _, consume) in enumerate(tasks):
        if i + ahead < len(tasks):
            copies[i + ahead].start()
        copies[i].wait()
        consume(i % STAGE_SLOTS)


def _block_kernel(x_ref, c_ref, w_ada_hbm, b_ada_ref, ng_ref, w_in_hbm, alng_ref, alnb_ref, aws_ref, abs_row_ref,
                  wg2_ref, bbg_ref, bng_ref, wpa_hbm, wpb_hbm, bm_ref, wout_hbm, fg_ref,
                  o_ref, state_ref, scores_ref, w_in_ref, w_sq_ref, mod_ref, abs_ref, stage_ref, sem_ref):
    @pl.when((pl.program_id(0) == 0) & (pl.program_id(1) == 0))
    def _():
        _stage_weights(w_in_hbm, (wpa_hbm, wpb_hbm, wout_hbm), w_ada_hbm, c_ref, b_ada_ref,
                       w_in_ref, w_sq_ref, mod_ref, stage_ref, sem_ref)
        for g in range(A_GROUPS):
            abs_ref[g] = jnp.transpose(jnp.broadcast_to(abs_row_ref[g:g + 1, :], (A_GROUP_DIM, A_CHUNK)))

    @pl.when(pl.program_id(1) == 0)
    def _():
        state_ref[...] = jnp.zeros_like(state_ref)

    tm = x_ref.shape[1]
    x = x_ref[0]
    mod = mod_ref[pl.ds(pl.program_id(0), 1), :]
    shift = mod[:, 0:D_MODEL]
    scale = mod[:, D_MODEL:2 * D_MODEL]
    gate = mod[:, 2 * D_MODEL:3 * D_MODEL]

    hn_b = (_rms(x) * (ng_ref[...] * (1.0 + scale)) + shift).astype(BF16)

    def proj(lo, width, w_ref=w_in_ref):
        return _dot(hn_b, w_ref[:, lo:lo + width])

    g1 = proj(OFF_G1, LANES)
    q_all = proj(OFF_Q, B_KEY_DIM) * (B_HEAD_K ** -0.5)
    k_all = proj(OFF_K, B_KEY_DIM)
    wg2 = jnp.concatenate([wg2_ref[...], jnp.zeros((LANES - B_GATE_RANK, B_KEY_DIM), F32)], axis=0)
    gk = _dot(g1.astype(BF16), wg2.astype(BF16)) + bbg_ref[...]
    p_bv = proj(OFF_BV, B_VAL_DIM)
    log2_a = (jnp.minimum(gk, 0.0) - jnp.log1p(jnp.exp(-jnp.abs(gk)))) * (LOG2E / B_GATE_NORMALIZER)
    cc = GLA_CHUNK
    gla_chunks = tm // cc
    tril_b = (lax.broadcasted_iota(jnp.int32, (cc, cc), 0)
              >= lax.broadcasted_iota(jnp.int32, (cc, cc), 1)).astype(BF16)
    la_hi, la_lo = _split_bf16(log2_a)
    p_u = proj(OFF_U, D_MODEL)
    b_chunks = [_dot(tril_b, la_hi[ci * cc:(ci + 1) * cc]) + _dot(tril_b, la_lo[ci * cc:(ci + 1) * cc])
                for ci in range(gla_chunks)]
    v_all = p_bv.astype(BF16)

    head_args = []
    for ci in range(gla_chunks):
        rs = slice(ci * cc, (ci + 1) * cc)
        for h in range(B_HEADS):
            ks = slice(h * B_HEAD_K, (h + 1) * B_HEAD_K)
            head_args.append((q_all[rs, ks], k_all[rs, ks], b_chunks[ci][:, ks],
                              v_all[rs, h * B_HEAD_V:(h + 1) * B_HEAD_V]))

    p_v = proj(OFF_V, D_MODEL)
    o_inter = [_gla_inter_chunk(qs, k, v_b, b, state_ref, i % B_HEADS)
               for i, (qs, k, b, v_b) in enumerate(head_args)]

    p_z = proj(OFF_Z, D_MODEL)
    a_u = _gelu(p_u)
    a_v = _gelu(p_v)
    p_ma = proj(OFF_MA, D_MODEL)
    a_vc = a_v - jnp.mean(a_v, axis=-1, keepdims=True)
    a_vn = (a_vc * lax.rsqrt(jnp.mean(a_vc * a_vc, axis=-1, keepdims=True) + EPS)
            * alng_ref[...] + alnb_ref[...]).astype(BF16)

    tri = (lax.broadcasted_iota(jnp.int32, (A_CHUNK, A_CHUNK), 0)
           >= lax.broadcasted_iota(jnp.int32, (A_CHUNK, A_CHUNK), 1))
    n_chunks = tm // A_CHUNK
    mixed_g = []
    for g in range(A_GROUPS):
        cols = slice(g * A_GROUP_DIM, (g + 1) * A_GROUP_DIM)
        v_g = jnp.concatenate([a_vn[ci * A_CHUNK:(ci + 1) * A_CHUNK, cols] for ci in range(n_chunks)], axis=1)
        m_g = _dot(jnp.where(tri, aws_ref[g], 0.0).astype(BF16), v_g)
        mixed_g.append(jnp.concatenate(
            [m_g[:, ci * A_GROUP_DIM:(ci + 1) * A_GROUP_DIM] for ci in range(n_chunks)], axis=0) + jnp.concatenate(
            [abs_ref[g]] * n_chunks, axis=0))
    mixed = jnp.concatenate(mixed_g, axis=1)
    y_a = a_u * mixed * _silu(p_z)
    g_a = jax.nn.sigmoid(p_ma + bm_ref[:, 0:D_MODEL])
    merged = g_a * _dot(y_a.astype(BF16), w_sq_ref[0, :, 0:D_MODEL])

    for i, (qs, k, b, _) in enumerate(head_args):
        scores_ref[i] = _scores_single_reference(qs, k, b)
    span = jnp.zeros((1, B_KEY_DIM), F32)
    for b in b_chunks:
        b_mid = b[cc // 2 - 1:cc // 2, :]
        span = jnp.maximum(span, jnp.maximum(-b_mid, b_mid - b[cc - 1:cc, :]))
    wide_span = jnp.max(span) >= FAST_SPAN_LOG2

    @pl.when(wide_span)
    def _():
        for i, (qs, k, b, _) in enumerate(head_args):
            scores_ref[i] = _scores_hierarchical(qs, k, b)

    p_bz = proj(OFF_BZ, B_VAL_DIM)
    p_mb = proj(OFF_MB, D_MODEL)
    for ci in range(gla_chunks):
        rs = slice(ci * cc, (ci + 1) * cc)
        heads = []
        for i in range(ci * B_HEADS, (ci + 1) * B_HEADS):
            o_h = o_inter[i] + _dot(scores_ref[i], head_args[i][3])
            heads.append(_rms(o_h) * bng_ref[...])
        o_b = jnp.concatenate(heads, axis=1) * _silu(p_bz[rs])
        y_b = _dot(o_b.astype(BF16), w_sq_ref[1, :, 0:D_MODEL])
        g_b = jax.nn.sigmoid(p_mb[rs] + bm_ref[:, D_MODEL:2 * D_MODEL])
        merged_c = merged[rs] + g_b * y_b
        out = _dot(merged_c.astype(BF16), w_sq_ref[2, :, 0:D_MODEL])
        o_ref[0, rs, :] = _rms(x_ref[0, rs, :] + gate * out) * fg_ref[...]


def _const_spec(shape):
    nd = len(shape)
    return pl.BlockSpec(shape, lambda b, j: (0,) * nd, pipeline_mode=pl.Buffered(1))


def kernel(x, c, w_ada, b_ada, norm_g, w_in, a_ln_g, a_ln_b, a_w_s, a_b_s, b_w_gate2, b_b_gate, b_norm_g, w_proj_a, w_proj_b, b_merge, w_out, final_g):
    bsz, seq, d = x.shape
    assert d == D_MODEL and seq % TILE_M == 0 and w_ada.shape[0] == 1 and bsz <= SUBLANES
    l = 0

    row2 = lambda v: v.reshape(1, -1)
    in_hbm = {2, 5, 13, 14, 16}
    operands = (
        x, c, w_ada, b_ada, row2(norm_g[l]), jnp.swapaxes(w_in, 1, 2), row2(a_ln_g[l]), row2(a_ln_b[l]),
        a_w_s[l], a_b_s[l], b_w_gate2[l], row2(b_b_gate[l]), row2(b_norm_g[l]), w_proj_a, w_proj_b,
        row2(b_merge[l]), w_out, row2(final_g),
    )
    in_specs = [pl.BlockSpec((1, TILE_M, d), lambda b, j: (b, j, 0))] + [
        pl.BlockSpec(memory_space=pl.ANY) if i in in_hbm else _const_spec(op.shape)
        for i, op in enumerate(operands) if i >= 1]

    return pl.pallas_call(
        _block_kernel,
        grid=(bsz, seq // TILE_M),
        in_specs=in_specs,
        out_specs=pl.BlockSpec((1, TILE_M, d), lambda b, j: (b, j, 0)),
        out_shape=jax.ShapeDtypeStruct((bsz, seq, d), x.dtype),
        scratch_shapes=[pltpu.VMEM((B_HEADS, B_HEAD_K, B_HEAD_V), F32),
                        pltpu.VMEM((TILE_M // GLA_CHUNK * B_HEADS, GLA_CHUNK, GLA_CHUNK), BF16),
                        pltpu.VMEM((d, W_IN_RESIDENT), BF16),
                        pltpu.VMEM((3, d, W_SQ_RESIDENT), BF16),
                        pltpu.VMEM((SUBLANES, 3 * d), F32),
                        pltpu.VMEM((A_GROUPS, A_CHUNK, A_GROUP_DIM), F32),
                        pltpu.VMEM((STAGE_SLOTS, STAGE_ROWS, D_MODEL), F32),
                        pltpu.SemaphoreType.DMA((STAGE_SLOTS,))],
        compiler_params=pltpu.CompilerParams(
            dimension_semantics=("arbitrary", "arbitrary"),
            vmem_limit_bytes=VMEM_LIMIT_BYTES),
        name="hybrid_block",
    )(*operands)
```

```python
import math

import jax
import jax.numpy as jnp
from jax import lax
from jax.experimental import pallas as pl
from jax.experimental.pallas import tpu as pltpu

D_MODEL = 1024
A_GROUPS = 8
A_CHUNK = 128
A_GROUP_DIM = D_MODEL // A_GROUPS
B_HEADS = 4
B_KEY_DIM = D_MODEL // 2
B_VAL_DIM = D_MODEL
B_HEAD_K = B_KEY_DIM // B_HEADS
B_HEAD_V = B_VAL_DIM // B_HEADS
B_GATE_RANK = 16
B_GATE_NORMALIZER = 16.0
EPS = 1e-6

LANES = 128
SUBLANES = 8
GLA_CHUNK = 256
TILE_M = 512
VMEM_LIMIT_BYTES = 60 * 1024 * 1024

OFF_U, OFF_V, OFF_Z = 0, 1024, 2048
OFF_Q, OFF_K = 3072, 3584
OFF_BV, OFF_BZ = 4096, 5120
W_MAIN = 6 * D_MODEL
IN_WIDTH = W_MAIN + B_GATE_RANK + 2 * D_MODEL
OFF_G1 = W_MAIN
OFF_MA, OFF_MB = W_MAIN + LANES, W_MAIN + LANES + D_MODEL
W_IN_RESIDENT = OFF_MB + D_MODEL
W_SQ_RESIDENT = D_MODEL + LANES
STAGE_ROWS = 128
STAGE_SLOTS = 8

F32 = jnp.float32
BF16 = jnp.bfloat16
LOG2E = 1.4426950408889634
FAST_SPAN_LOG2 = 80.0


def _dot(a, b):
    return jnp.dot(a, b, preferred_element_type=F32)


def _dot_nt(a, b):
    return lax.dot_general(a, b, (((1,), (1,)), ((), ())), preferred_element_type=F32)


def _dot_tn(a, b):
    return lax.dot_general(a, b, (((0,), (0,)), ((), ())), preferred_element_type=F32)


def _split_bf16(a):
    hi = a.astype(BF16)
    lo = (a - hi.astype(F32)).astype(BF16)
    return hi, lo


def _gelu(a):
    return 0.5 * a * (1.0 + lax.erf(a * (1.0 / math.sqrt(2.0))))


def _silu(a):
    return a * jax.nn.sigmoid(a)


def _rms(a):
    return a * lax.rsqrt(jnp.mean(a * a, axis=-1, keepdims=True) + EPS)


def _bcast_row_in_blocks(a, blk, row):
    m, n = a.shape
    a3 = a.reshape(m // blk, blk, n)
    return jnp.broadcast_to(a3[:, row:row + 1, :], (m // blk, blk, n)).reshape(m, n)


def _scores_single_reference(qs, k, b):
    c = qs.shape[0]
    ref_b = b[c // 2 - 1:c // 2, :]
    s = _dot_nt((qs * jnp.exp2(b - ref_b)).astype(BF16), (k * jnp.exp2(ref_b - b)).astype(BF16))
    causal = (lax.broadcasted_iota(jnp.int32, (c, c), 0) >= lax.broadcasted_iota(jnp.int32, (c, c), 1))
    return jnp.where(causal, s.astype(BF16), jnp.zeros((), BF16))


def _scores_hierarchical(qs, k, b):
    c = qs.shape[0]
    row = lax.broadcasted_iota(jnp.int32, (c, 1), 0)
    col = lax.broadcasted_iota(jnp.int32, (1, c), 1)
    scores = jnp.zeros((c, c), F32)
    m = SUBLANES
    while m < c:
        ref_b = _bcast_row_in_blocks(b, 2 * m, m - 1)
        e = jnp.exp2(-jnp.abs(b - ref_b))
        siblings = ((row // m) == (col // m) + 1) & ((col // m) % 2 == 0)
        scores = jnp.where(siblings, _dot_nt((qs * e).astype(BF16), (k * e).astype(BF16)), scores)
        m *= 2
    row_in = row % SUBLANES
    lane = lax.broadcasted_iota(jnp.int32, (1, LANES), 1)
    diag = jnp.zeros((c, LANES), F32)
    for j in range(SUBLANES):
        b_j = _bcast_row_in_blocks(b, SUBLANES, j)
        k_j = _bcast_row_in_blocks(k, SUBLANES, j)
        s_j = jnp.sum(qs * k_j * jnp.exp2(jnp.minimum(b - b_j, 0.0)), axis=-1, keepdims=True)
        diag = jnp.where((lane == j) & (row_in >= j), s_j, diag)
    expand = (lax.broadcasted_iota(jnp.int32, (LANES, c), 0)
              == lax.broadcasted_iota(jnp.int32, (LANES, c), 1) % SUBLANES).astype(BF16)
    diag_full = _dot(diag.astype(BF16), expand)
    same_block = (row // SUBLANES) == (col // SUBLANES)
    return jnp.where(same_block, diag_full.astype(BF16), scores.astype(BF16))


def _gla_inter_chunk(qs, k, v_b, b, state_ref, h):
    c = qs.shape[0]
    state = state_ref[h]
    o = _dot((qs * jnp.exp2(b)).astype(BF16), state.astype(BF16))
    b_last = b[c - 1:c, :]
    k_dec = (k * jnp.exp2(b_last - b)).astype(BF16)
    decay_col = jnp.transpose(jnp.broadcast_to(jnp.exp2(b_last), (LANES, B_HEAD_K)))[:, 0:1]
    state_ref[h] = decay_col * state + _dot_tn(k_dec, v_b)
    return o


def _stage_weights(w_in_t_hbm, sq_hbm, w_ada_hbm, c_ref, b_ada_ref, w_in_s, w_sq_s, mod_s, stage_ref, sem_ref):
    tasks = []

    bsz = c_ref.shape[0]
    act = _silu(c_ref[...])
    act = jnp.concatenate([act, jnp.zeros((SUBLANES - bsz, D_MODEL), F32)], axis=0)
    mod_s[...] = jnp.broadcast_to(b_ada_ref[...], mod_s.shape)
    for c0 in range(0, 3 * D_MODEL, D_MODEL):
        for r0 in range(0, D_MODEL, STAGE_ROWS):
            def consume(slot, c0=c0, r0=r0):
                a_hi, a_lo = _split_bf16(act[:, r0:r0 + STAGE_ROWS])
                w_hi, w_lo = _split_bf16(stage_ref[slot])
                mod_s[:, c0:c0 + D_MODEL] += _dot(a_hi, w_hi) + (_dot(a_lo, w_hi) + _dot(a_hi, w_lo))
            tasks.append((w_ada_hbm.at[0, pl.ds(r0, STAGE_ROWS), pl.ds(c0, D_MODEL)], STAGE_ROWS, consume))

    def in_task(src_row0, rows, dst_col0, valid):
        def consume(slot):
            blk = jnp.transpose(stage_ref[slot, 0:rows, :].astype(BF16))
            if valid < rows:
                blk = jnp.where(lax.broadcasted_iota(jnp.int32, blk.shape, 1) < valid, blk, jnp.zeros((), BF16))
            w_in_s[:, dst_col0:dst_col0 + rows] = blk
        tasks.append((w_in_t_hbm.at[0, pl.ds(src_row0, rows), :], rows, consume))

    for c0 in range(0, W_MAIN, STAGE_ROWS):
        in_task(c0, STAGE_ROWS, c0, STAGE_ROWS)
    in_task(W_MAIN, LANES, OFF_G1, B_GATE_RANK)
    for c0 in range(0, 2 * D_MODEL, STAGE_ROWS):
        in_task(W_MAIN + B_GATE_RANK + c0, STAGE_ROWS, OFF_MA + c0, STAGE_ROWS)
    for m, w_hbm in enumerate(sq_hbm):
        for r0 in range(0, D_MODEL, STAGE_ROWS):
            def consume(slot, m=m, r0=r0):
                w_sq_s[m, r0:r0 + STAGE_ROWS, 0:D_MODEL] = stage_ref[slot].astype(BF16)
            tasks.append((w_hbm.at[0, pl.ds(r0, STAGE_ROWS), :], STAGE_ROWS, consume))

    copies = [pltpu.make_async_copy(src, stage_ref.at[i % STAGE_SLOTS, pl.ds(0, rows), :],
                                    sem_ref.at[i % STAGE_SLOTS])
              for i, (src, rows, _) in enumerate(tasks)]
    ahead = STAGE_SLOTS - 1
    for cp in copies[:ahead]:
        cp.start()
    for i, (_, _, consume) in enumerate(tasks):
        if i + ahead < len(tasks):
            copies[i + ahead].start()
        copies[i].wait()
        consume(i % STAGE_SLOTS)


def _block_kernel(x_ref, c_ref, w_ada_hbm, b_ada_ref, ng_ref, w_in_hbm, alng_ref, alnb_ref, aws_ref, abs_row_ref,
                  wg2_ref, bbg_ref, bng_ref, wpa_hbm, wpb_hbm, bm_ref, wout_hbm, fg_ref,
                  o_ref, state_ref, scores_ref, w_in_ref, w_sq_ref, mod_ref, abs_ref, stage_ref, sem_ref):
    @pl.when((pl.program_id(0) == 0) & (pl.program_id(1) == 0))
    def _():
        _stage_weights(w_in_hbm, (wpa_hbm, wpb_hbm, wout_hbm), w_ada_hbm, c_ref, b_ada_ref,
                       w_in_ref, w_sq_ref, mod_ref, stage_ref, sem_ref)
        for g in range(A_GROUPS):
            abs_ref[g] = jnp.transpose(jnp.broadcast_to(abs_row_ref[g:g + 1, :], (A_GROUP_DIM, A_CHUNK)))

    @pl.when(pl.program_id(1) == 0)
    def _():
        state_ref[...] = jnp.zeros_like(state_ref)

    tm = x_ref.shape[1]
    x = x_ref[0]
    mod = mod_ref[pl.ds(pl.program_id(0), 1), :]
    shift = mod[:, 0:D_MODEL]
    scale = mod[:, D_MODEL:2 * D_MODEL]
    gate = mod[:, 2 * D_MODEL:3 * D_MODEL]

    hn_b = (_rms(x) * (ng_ref[...] * (1.0 + scale)) + shift).astype(BF16)

    def proj(lo, width, w_ref=w_in_ref):
        return _dot(hn_b, w_ref[:, lo:lo + width])

    g1 = proj(OFF_G1, LANES)
    q_all = proj(OFF_Q, B_KEY_DIM) * (B_HEAD_K ** -0.5)
    k_all = proj(OFF_K, B_KEY_DIM)
    wg2 = jnp.concatenate([wg2_ref[...], jnp.zeros((LANES - B_GATE_RANK, B_KEY_DIM), F32)], axis=0)
    gk = _dot(g1.astype(BF16), wg2.astype(BF16)) + bbg_ref[...]
    p_bv = proj(OFF_BV, B_VAL_DIM)
    log2_a = (jnp.minimum(gk, 0.0) - jnp.log1p(jnp.exp(-jnp.abs(gk)))) * (LOG2E / B_GATE_NORMALIZER)
    cc = GLA_CHUNK
    gla_chunks = tm // cc
    tril_b = (lax.broadcasted_iota(jnp.int32, (cc, cc), 0)
              >= lax.broadcasted_iota(jnp.int32, (cc, cc), 1)).astype(BF16)
    la_hi, la_lo = _split_bf16(log2_a)
    p_u = proj(OFF_U, D_MODEL)
    b_chunks = [_dot(tril_b, la_hi[ci * cc:(ci + 1) * cc]) + _dot(tril_b, la_lo[ci * cc:(ci + 1) * cc])
                for ci in range(gla_chunks)]
    v_all = p_bv.astype(BF16)

    head_args = []
    for ci in range(gla_chunks):
        rs = slice(ci * cc, (ci + 1) * cc)
        for h in range(B_HEADS):
            ks = slice(h * B_HEAD_K, (h + 1) * B_HEAD_K)
            head_args.append((q_all[rs, ks], k_all[rs, ks], b_chunks[ci][:, ks],
                              v_all[rs, h * B_HEAD_V:(h + 1) * B_HEAD_V]))

    p_v = proj(OFF_V, D_MODEL)
    o_inter = [_gla_inter_chunk(qs, k, v_b, b, state_ref, i % B_HEADS)
               for i, (qs, k, b, v_b) in enumerate(head_args)]

    p_z = proj(OFF_Z, D_MODEL)
    a_u = _gelu(p_u)
    a_v = _gelu(p_v)
    p_ma = proj(OFF_MA, D_MODEL)
    a_vc = a_v - jnp.mean(a_v, axis=-1, keepdims=True)
    a_vn = (a_vc * lax.rsqrt(jnp.mean(a_vc * a_vc, axis=-1, keepdims=True) + EPS)
            * alng_ref[...] + alnb_ref[...]).astype(BF16)

    tri = (lax.broadcasted_iota(jnp.int32, (A_CHUNK, A_CHUNK), 0)
           >= lax.broadcasted_iota(jnp.int32, (A_CHUNK, A_CHUNK), 1))
    n_chunks = tm // A_CHUNK
    mixed_g = []
    for g in range(A_GROUPS):
        cols = slice(g * A_GROUP_DIM, (g + 1) * A_GROUP_DIM)
        v_g = jnp.concatenate([a_vn[ci * A_CHUNK:(ci + 1) * A_CHUNK, cols] for ci in range(n_chunks)], axis=1)
        m_g = _dot(jnp.where(tri, aws_ref[g], 0.0).astype(BF16), v_g)
        mixed_g.append(jnp.concatenate(
            [m_g[:, ci * A_GROUP_DIM:(ci + 1) * A_GROUP_DIM] for ci in range(n_chunks)], axis=0) + jnp.concatenate(
            [abs_ref[g]] * n_chunks, axis=0))
    mixed = jnp.concatenate(mixed_g, axis=1)
    y_a = a_u * mixed * _silu(p_z)
    g_a = jax.nn.sigmoid(p_ma + bm_ref[:, 0:D_MODEL])
    merged = g_a * _dot(y_a.astype(BF16), w_sq_ref[0, :, 0:D_MODEL])

    for i, (qs, k, b, _) in enumerate(head_args):
        scores_ref[i] = _scores_single_reference(qs, k, b)
    span = jnp.zeros((1, B_KEY_DIM), F32)
    for b in b_chunks:
        b_mid = b[cc // 2 - 1:cc // 2, :]
        span = jnp.maximum(span, jnp.maximum(-b_mid, b_mid - b[cc - 1:cc, :]))
    wide_span = jnp.max(span) >= FAST_SPAN_LOG2

    @pl.when(wide_span)
    def _():
        for i, (qs, k, b, _) in enumerate(head_args):
            scores_ref[i] = _scores_hierarchical(qs, k, b)

    p_bz = proj(OFF_BZ, B_VAL_DIM)
    p_mb = proj(OFF_MB, D_MODEL)
    for ci in range(gla_chunks):
        rs = slice(ci * cc, (ci + 1) * cc)
        heads = []
        for i in range(ci * B_HEADS, (ci + 1) * B_HEADS):
            o_h = o_inter[i] + _dot(scores_ref[i], head_args[i][3])
            heads.append(_rms(o_h) * bng_ref[...])
        o_b = jnp.concatenate(heads, axis=1) * _silu(p_bz[rs])
        y_b = _dot(o_b.astype(BF16), w_sq_ref[1, :, 0:D_MODEL])
        g_b = jax.nn.sigmoid(p_mb[rs] + bm_ref[:, D_MODEL:2 * D_MODEL])
        merged_c = merged[rs] + g_b * y_b
        out = _dot(merged_c.astype(BF16), w_sq_ref[2, :, 0:D_MODEL])
        o_ref[0, rs, :] = _rms(x_ref[0, rs, :] + gate * out) * fg_ref[...]


def _const_spec(shape):
    nd = len(shape)
    return pl.BlockSpec(shape, lambda b, j: (0,) * nd, pipeline_mode=pl.Buffered(1))


def kernel(x, c, w_ada, b_ada, norm_g, w_in, a_ln_g, a_ln_b, a_w_s, a_b_s, b_w_gate2, b_b_gate, b_norm_g, w_proj_a, w_proj_b, b_merge, w_out, final_g):
    bsz, seq, d = x.shape
    assert d == D_MODEL and seq % TILE_M == 0 and w_ada.shape[0] == 1 and bsz <= SUBLANES
    l = 0

    row2 = lambda v: v.reshape(1, -1)
    in_hbm = {2, 5, 13, 14, 16}
    operands = (
        x, c, w_ada, b_ada, row2(norm_g[l]), jnp.swapaxes(w_in, 1, 2), row2(a_ln_g[l]), row2(a_ln_b[l]),
        a_w_s[l], a_b_s[l], b_w_gate2[l], row2(b_b_gate[l]), row2(b_norm_g[l]), w_proj_a, w_proj_b,
        row2(b_merge[l]), w_out, row2(final_g),
    )
    in_specs = [pl.BlockSpec((1, TILE_M, d), lambda b, j: (b, j, 0))] + [
        pl.BlockSpec(memory_space=pl.ANY) if i in in_hbm else _const_spec(op.shape)
        for i, op in enumerate(operands) if i >= 1]

    return pl.pallas_call(
        _block_kernel,
        grid=(bsz, seq // TILE_M),
        in_specs=in_specs,
        out_specs=pl.BlockSpec((1, TILE_M, d), lambda b, j: (b, j, 0)),
        out_shape=jax.ShapeDtypeStruct((bsz, seq, d), x.dtype),
        scratch_shapes=[pltpu.VMEM((B_HEADS, B_HEAD_K, B_HEAD_V), F32),
                        pltpu.VMEM((TILE_M // GLA_CHUNK * B_HEADS, GLA_CHUNK, GLA_CHUNK), BF16),
                        pltpu.VMEM((d, W_IN_RESIDENT), BF16),
                        pltpu.VMEM((3, d, W_SQ_RESIDENT), BF16),
                        pltpu.VMEM((SUBLANES, 3 * d), F32),
                        pltpu.VMEM((A_GROUPS, A_CHUNK, A_GROUP_DIM), F32),
                        pltpu.VMEM((STAGE_SLOTS, STAGE_ROWS, D_MODEL), F32),
                        pltpu.SemaphoreType.DMA((STAGE_SLOTS,))],
        compiler_params=pltpu.CompilerParams(
            dimension_semantics=("arbitrary", "arbitrary"),
            vmem_limit_bytes=VMEM_LIMIT_BYTES),
        name="hybrid_block",
    )(*operands)
```

```python
import math

import jax
import jax.numpy as jnp
from jax import lax
from jax.experimental import pallas as pl
from jax.experimental.pallas import tpu as pltpu

D_MODEL = 1024
A_GROUPS = 8
A_CHUNK = 128
A_GROUP_DIM = D_MODEL // A_GROUPS
B_HEADS = 4
B_KEY_DIM = D_MODEL // 2
B_VAL_DIM = D_MODEL
B_HEAD_K = B_KEY_DIM // B_HEADS
B_HEAD_V = B_VAL_DIM // B_HEADS
B_GATE_RANK = 16
B_GATE_NORMALIZER = 16.0
EPS = 1e-6

LANES = 128
SUBLANES = 8
GLA_CHUNK = 256
TILE_M = 512
VMEM_LIMIT_BYTES = 60 * 1024 * 1024

OFF_U, OFF_V, OFF_Z = 0, 1024, 2048
OFF_Q, OFF_K = 3072, 3584
OFF_BV, OFF_BZ = 4096, 5120
W_MAIN = 6 * D_MODEL
IN_WIDTH = W_MAIN + B_GATE_RANK + 2 * D_MODEL
OFF_G1 = W_MAIN
OFF_MA, OFF_MB = W_MAIN + LANES, W_MAIN + LANES + D_MODEL
W_IN_RESIDENT = OFF_MB + D_MODEL
W_SQ_RESIDENT = D_MODEL + LANES
STAGE_ROWS = 128
STAGE_SLOTS = 8

F32 = jnp.float32
BF16 = jnp.bfloat16
LOG2E = 1.4426950408889634
FAST_SPAN_LOG2 = 80.0


def _dot(a, b):
    return jnp.dot(a, b, preferred_element_type=F32)


def _dot_nt(a, b):
    return lax.dot_general(a, b, (((1,), (1,)), ((), ())), preferred_element_type=F32)


def _dot_tn(a, b):
    return lax.dot_general(a, b, (((0,), (0,)), ((), ())), preferred_element_type=F32)


def _split_bf16(a):
    hi = a.astype(BF16)
    lo = (a - hi.astype(F32)).astype(BF16)
    return hi, lo


def _gelu(a):
    return 0.5 * a * (1.0 + lax.erf(a * (1.0 / math.sqrt(2.0))))


def _silu(a):
    return a * jax.nn.sigmoid(a)


def _rms(a):
    return a * lax.rsqrt(jnp.mean(a * a, axis=-1, keepdims=True) + EPS)


def _bcast_row_in_blocks(a, blk, row):
    m, n = a.shape
    a3 = a.reshape(m // blk, blk, n)
    return jnp.broadcast_to(a3[:, row:row + 1, :], (m // blk, blk, n)).reshape(m, n)


def _scores_single_reference(qs, k, b):
    c = qs.shape[0]
    ref_b = b[c // 2 - 1:c // 2, :]
    s = _dot_nt((qs * jnp.exp2(b - ref_b)).astype(BF16), (k * jnp.exp2(ref_b - b)).astype(BF16))
    causal = (lax.broadcasted_iota(jnp.int32, (c, c), 0) >= lax.broadcasted_iota(jnp.int32, (c, c), 1))
    return jnp.where(causal, s.astype(BF16), jnp.zeros((), BF16))


def _scores_hierarchical(qs, k, b):
    c = qs.shape[0]
    row = lax.broadcasted_iota(jnp.int32, (c, 1), 0)
    col = lax.broadcasted_iota(jnp.int32, (1, c), 1)
    scores = jnp.zeros((c, c), F32)
    m = SUBLANES
    while m < c:
        ref_b = _bcast_row_in_blocks(b, 2 * m, m - 1)
        e = jnp.exp2(-jnp.abs(b - ref_b))
        siblings = ((row // m) == (col // m) + 1) & ((col // m) % 2 == 0)
        scores = jnp.where(siblings, _dot_nt((qs * e).astype(BF16), (k * e).astype(BF16)), scores)
        m *= 2
    row_in = row % SUBLANES
    lane = lax.broadcasted_iota(jnp.int32, (1, LANES), 1)
    diag = jnp.zeros((c, LANES), F32)
    for j in range(SUBLANES):
        b_j = _bcast_row_in_blocks(b, SUBLANES, j)
        k_j = _bcast_row_in_blocks(k, SUBLANES, j)
        s_j = jnp.sum(qs * k_j * jnp.exp2(jnp.minimum(b - b_j, 0.0)), axis=-1, keepdims=True)
        diag = jnp.where((lane == j) & (row_in >= j), s_j, diag)
    expand = (lax.broadcasted_iota(jnp.int32, (LANES, c), 0)
              == lax.broadcasted_iota(jnp.int32, (LANES, c), 1) % SUBLANES).astype(BF16)
    diag_full = _dot(diag.astype(BF16), expand)
    same_block = (row // SUBLANES) == (col // SUBLANES)
    return jnp.where(same_block, diag_full.astype(BF16), scores.astype(BF16))


def _gla_inter_chunk(qs, k, v_b, b, state_ref, h):
    c = qs.shape[0]
    state = state_ref[h]
    o = _dot((qs * jnp.exp2(b)).astype(BF16), state.astype(BF16))
    b_last = b[c - 1:c, :]
    k_dec = (k * jnp.exp2(b_last - b)).astype(BF16)
    decay_col = jnp.transpose(jnp.broadcast_to(jnp.exp2(b_last), (LANES, B_HEAD_K)))[:, 0:1]
    state_ref[h] = decay_col * state + _dot_tn(k_dec, v_b)
    return o


def _stage_weights(w_in_t_hbm, sq_hbm, w_ada_hbm, c_ref, b_ada_ref, w_in_s, w_sq_s, mod_s, stage_ref, sem_ref):
    tasks = []

    bsz = c_ref.shape[0]
    act = _silu(c_ref[...])
    act = jnp.concatenate([act, jnp.zeros((SUBLANES - bsz, D_MODEL), F32)], axis=0)
    mod_s[...] = jnp.broadcast_to(b_ada_ref[...], mod_s.shape)
    for c0 in range(0, 3 * D_MODEL, D_MODEL):
        for r0 in range(0, D_MODEL, STAGE_ROWS):
            def consume(slot, c0=c0, r0=r0):
                a_hi, a_lo = _split_bf16(act[:, r0:r0 + STAGE_ROWS])
                w_hi, w_lo = _split_bf16(stage_ref[slot])
                mod_s[:, c0:c0 + D_MODEL] += _dot(a_hi, w_hi) + (_dot(a_lo, w_hi) + _dot(a_hi, w_lo))
            tasks.append((w_ada_hbm.at[0, pl.ds(r0, STAGE_ROWS), pl.ds(c0, D_MODEL)], STAGE_ROWS, consume))

    def in_task(src_row0, rows, dst_col0, valid):
        def consume(slot):
            blk = jnp.transpose(stage_ref[slot, 0:rows, :])
            if valid < rows:
                blk = jnp.where(lax.broadcasted_iota(jnp.int32, blk.shape, 1) < valid, blk, 0.0)
            w_in_s[:, dst_col0:dst_col0 + rows] = blk.astype(BF16)
        tasks.append((w_in_t_hbm.at[0, pl.ds(src_row0, rows), :], rows, consume))

    for c0 in range(0, W_MAIN, STAGE_ROWS):
        in_task(c0, STAGE_ROWS, c0, STAGE_ROWS)
    in_task(W_MAIN, LANES, OFF_G1, B_GATE_RANK)
    for c0 in range(0, 2 * D_MODEL, STAGE_ROWS):
        in_task(W_MAIN + B_GATE_RANK + c0, STAGE_ROWS, OFF_MA + c0, STAGE_ROWS)
    for m, w_hbm in enumerate(sq_hbm):
        for r0 in range(0, D_MODEL, STAGE_ROWS):
            def consume(slot, m=m, r0=r0):
                w_sq_s[m, r0:r0 + STAGE_ROWS, 0:D_MODEL] = stage_ref[slot].astype(BF16)
            tasks.append((w_hbm.at[0, pl.ds(r0, STAGE_ROWS), :], STAGE_ROWS, consume))

    copies = [pltpu.make_async_copy(src, stage_ref.at[i % STAGE_SLOTS, pl.ds(0, rows), :],
                                    sem_ref.at[i % STAGE_SLOTS])
              for i, (src, rows, _) in enumerate(tasks)]
    ahead = STAGE_SLOTS - 1
    for cp in copies[:ahead]:
        cp.start()
    for i, (_, _, consume) in enumerate(tasks):
        if i + ahead < len(tasks):
            copies[i + ahead].start()
        copies[i].wait()
        consume(i % STAGE_SLOTS)


def _block_kernel(x_ref, c_ref, w_ada_hbm, b_ada_ref, ng_ref, w_in_hbm, alng_ref, alnb_ref, aws_ref, abs_row_ref,
                  wg2_ref, bbg_ref, bng_ref, wpa_hbm, wpb_hbm, bm_ref, wout_hbm, fg_ref,
                  o_ref, state_ref, scores_ref, w_in_ref, w_sq_ref, mod_ref, abs_ref, hn_ref, stage_ref, sem_ref):
    @pl.when((pl.program_id(0) == 0) & (pl.program_id(1) == 0))
    def _():
        _stage_weights(w_in_hbm, (wpa_hbm, wpb_hbm, wout_hbm), w_ada_hbm, c_ref, b_ada_ref,
                       w_in_ref, w_sq_ref, mod_ref, stage_ref, sem_ref)
        for g in range(A_GROUPS):
            abs_ref[g] = jnp.transpose(jnp.broadcast_to(abs_row_ref[g:g + 1, :], (A_GROUP_DIM, A_CHUNK)))

    @pl.when(pl.program_id(1) == 0)
    def _():
        state_ref[...] = jnp.zeros_like(state_ref)

    tm = x_ref.shape[1]
    x = x_ref[0]
    mod = mod_ref[pl.ds(pl.program_id(0), 1), :]
    shift = mod[:, 0:D_MODEL]
    scale = mod[:, D_MODEL:2 * D_MODEL]
    gate = mod[:, 2 * D_MODEL:3 * D_MODEL]

    hn_ref[...] = (_rms(x) * (ng_ref[...] * (1.0 + scale)) + shift).astype(BF16)

    def proj(lo, width, w_ref=w_in_ref):
        return _dot(hn_ref[...], w_ref[:, lo:lo + width])

    g1 = proj(OFF_G1, LANES)
    q_all = proj(OFF_Q, B_KEY_DIM) * (B_HEAD_K ** -0.5)
    k_all = proj(OFF_K, B_KEY_DIM)
    wg2 = jnp.concatenate([wg2_ref[...], jnp.zeros((LANES - B_GATE_RANK, B_KEY_DIM), F32)], axis=0)
    gk = _dot(g1.astype(BF16), wg2.astype(BF16)) + bbg_ref[...]
    p_bv = proj(OFF_BV, B_VAL_DIM)
    log2_a = (jnp.minimum(gk, 0.0) - jnp.log1p(jnp.exp(-jnp.abs(gk)))) * (LOG2E / B_GATE_NORMALIZER)
    cc = GLA_CHUNK
    gla_chunks = tm // cc
    tril_b = (lax.broadcasted_iota(jnp.int32, (cc, cc), 0)
              >= lax.broadcasted_iota(jnp.int32, (cc, cc), 1)).astype(BF16)
    la_hi, la_lo = _split_bf16(log2_a)
    p_u = proj(OFF_U, D_MODEL)
    b_chunks = [_dot(tril_b, la_hi[ci * cc:(ci + 1) * cc]) + _dot(tril_b, la_lo[ci * cc:(ci + 1) * cc])
                for ci in range(gla_chunks)]
    v_all = p_bv.astype(BF16)

    head_args = []
    for ci in range(gla_chunks):
        rs = slice(ci * cc, (ci + 1) * cc)
        for h in range(B_HEADS):
            ks = slice(h * B_HEAD_K, (h + 1) * B_HEAD_K)
            head_args.append((q_all[rs, ks], k_all[rs, ks], b_chunks[ci][:, ks],
                              v_all[rs, h * B_HEAD_V:(h + 1) * B_HEAD_V]))

    p_v = proj(OFF_V, D_MODEL)
    o_inter = [_gla_inter_chunk(qs, k, v_b, b, state_ref, i % B_HEADS)
               for i, (qs, k, b, v_b) in enumerate(head_args)]

    p_z = proj(OFF_Z, D_MODEL)
    a_u = _gelu(p_u)
    a_v = _gelu(p_v)
    p_ma = proj(OFF_MA, D_MODEL)
    a_vc = a_v - jnp.mean(a_v, axis=-1, keepdims=True)
    a_vn = (a_vc * lax.rsqrt(jnp.mean(a_vc * a_vc, axis=-1, keepdims=True) + EPS)
            * alng_ref[...] + alnb_ref[...]).astype(BF16)

    tri = (lax.broadcasted_iota(jnp.int32, (A_CHUNK, A_CHUNK), 0)
           >= lax.broadcasted_iota(jnp.int32, (A_CHUNK, A_CHUNK), 1))
    n_chunks = tm // A_CHUNK
    mixed_g = []
    for g in range(A_GROUPS):
        cols = slice(g * A_GROUP_DIM, (g + 1) * A_GROUP_DIM)
        v_g = jnp.concatenate([a_vn[ci * A_CHUNK:(ci + 1) * A_CHUNK, cols] for ci in range(n_chunks)], axis=1)
        m_g = _dot(jnp.where(tri, aws_ref[g], 0.0).astype(BF16), v_g)
        mixed_g.append(jnp.concatenate(
            [m_g[:, ci * A_GROUP_DIM:(ci + 1) * A_GROUP_DIM] for ci in range(n_chunks)], axis=0) + jnp.concatenate(
            [abs_ref[g]] * n_chunks, axis=0))
    mixed = jnp.concatenate(mixed_g, axis=1)
    y_a = a_u * mixed * _silu(p_z)
    g_a = jax.nn.sigmoid(p_ma + bm_ref[:, 0:D_MODEL])
    merged = g_a * _dot(y_a.astype(BF16), w_sq_ref[0, :, 0:D_MODEL])

    for i, (qs, k, b, _) in enumerate(head_args):
        scores_ref[i] = _scores_single_reference(qs, k, b)
    span = jnp.zeros((1, B_KEY_DIM), F32)
    for b in b_chunks:
        b_mid = b[cc // 2 - 1:cc // 2, :]
        span = jnp.maximum(span, jnp.maximum(-b_mid, b_mid - b[cc - 1:cc, :]))
    wide_span = jnp.max(span) >= FAST_SPAN_LOG2

    @pl.when(wide_span)
    def _():
        for i, (qs, k, b, _) in enumerate(head_args):
            scores_ref[i] = _scores_hierarchical(qs, k, b)

    p_bz = proj(OFF_BZ, B_VAL_DIM)
    p_mb = proj(OFF_MB, D_MODEL)
    for ci in range(gla_chunks):
        rs = slice(ci * cc, (ci + 1) * cc)
        heads = []
        for i in range(ci * B_HEADS, (ci + 1) * B_HEADS):
            o_h = o_inter[i] + _dot(scores_ref[i], head_args[i][3])
            heads.append(_rms(o_h) * bng_ref[...])
        o_b = jnp.concatenate(heads, axis=1) * _silu(p_bz[rs])
        y_b = _dot(o_b.astype(BF16), w_sq_ref[1, :, 0:D_MODEL])
        g_b = jax.nn.sigmoid(p_mb[rs] + bm_ref[:, D_MODEL:2 * D_MODEL])
        merged_c = merged[rs] + g_b * y_b
        out = _dot(merged_c.astype(BF16), w_sq_ref[2, :, 0:D_MODEL])
        o_ref[0, rs, :] = _rms(x_ref[0, rs, :] + gate * out) * fg_ref[...]


def _const_spec(shape):
    nd = len(shape)
    return pl.BlockSpec(shape, lambda b, j: (0,) * nd, pipeline_mode=pl.Buffered(1))


def kernel(x, c, w_ada, b_ada, norm_g, w_in, a_ln_g, a_ln_b, a_w_s, a_b_s, b_w_gate2, b_b_gate, b_norm_g, w_proj_a, w_proj_b, b_merge, w_out, final_g):
    bsz, seq, d = x.shape
    assert d == D_MODEL and seq % TILE_M == 0 and w_ada.shape[0] == 1 and bsz <= SUBLANES
    l = 0

    row2 = lambda v: v.reshape(1, -1)
    in_hbm = {2, 5, 13, 14, 16}
    operands = (
        x, c, w_ada, b_ada, row2(norm_g[l]), jnp.swapaxes(w_in, 1, 2), row2(a_ln_g[l]), row2(a_ln_b[l]),
        a_w_s[l], a_b_s[l], b_w_gate2[l], row2(b_b_gate[l]), row2(b_norm_g[l]), w_proj_a, w_proj_b,
        row2(b_merge[l]), w_out, row2(final_g),
    )
    in_specs = [pl.BlockSpec((1, TILE_M, d), lambda b, j: (b, j, 0))] + [
        pl.BlockSpec(memory_space=pl.ANY) if i in in_hbm else _const_spec(op.shape)
        for i, op in enumerate(operands) if i >= 1]

    return pl.pallas_call(
        _block_kernel,
        grid=(bsz, seq // TILE_M),
        in_specs=in_specs,
        out_specs=pl.BlockSpec((1, TILE_M, d), lambda b, j: (b, j, 0)),
        out_shape=jax.ShapeDtypeStruct((bsz, seq, d), x.dtype),
        scratch_shapes=[pltpu.VMEM((B_HEADS, B_HEAD_K, B_HEAD_V), F32),
                        pltpu.VMEM((TILE_M // GLA_CHUNK * B_HEADS, GLA_CHUNK, GLA_CHUNK), BF16),
                        pltpu.VMEM((d, W_IN_RESIDENT), BF16),
                        pltpu.VMEM((3, d, W_SQ_RESIDENT), BF16),
                        pltpu.VMEM((SUBLANES, 3 * d), F32),
                        pltpu.VMEM((A_GROUPS, A_CHUNK, A_GROUP_DIM), F32),
                        pltpu.VMEM((TILE_M, d), BF16),
                        pltpu.VMEM((STAGE_SLOTS, STAGE_ROWS, D_MODEL), F32),
                        pltpu.SemaphoreType.DMA((STAGE_SLOTS,))],
        compiler_params=pltpu.CompilerParams(
            dimension_semantics=("arbitrary", "arbitrary"),
            vmem_limit_bytes=VMEM_LIMIT_BYTES),
        name="hybrid_block",
    )(*operands)
```

```python
import math

import jax
import jax.numpy as jnp
from jax import lax
from jax.experimental import pallas as pl
from jax.experimental.pallas import tpu as pltpu

D_MODEL = 1024
A_GROUPS = 8
A_CHUNK = 128
A_GROUP_DIM = D_MODEL // A_GROUPS
B_HEADS = 4
B_KEY_DIM = D_MODEL // 2
B_VAL_DIM = D_MODEL
B_HEAD_K = B_KEY_DIM // B_HEADS
B_HEAD_V = B_VAL_DIM // B_HEADS
B_GATE_RANK = 16
B_GATE_NORMALIZER = 16.0
EPS = 1e-6

LANES = 128
SUBLANES = 8
GLA_CHUNK = 256
TILE_M = 512
VMEM_LIMIT_BYTES = 60 * 1024 * 1024

OFF_U, OFF_V, OFF_Z = 0, 1024, 2048
OFF_Q, OFF_K = 3072, 3584
OFF_BV, OFF_BZ = 4096, 5120
W_MAIN = 6 * D_MODEL
IN_WIDTH = W_MAIN + B_GATE_RANK + 2 * D_MODEL
OFF_G1 = W_MAIN
OFF_MA, OFF_MB = W_MAIN + LANES, W_MAIN + LANES + D_MODEL
W_IN_RESIDENT = OFF_MB + D_MODEL
W_SQ_RESIDENT = D_MODEL + LANES
STAGE_ROWS = 128
STAGE_SLOTS = 8

F32 = jnp.float32
BF16 = jnp.bfloat16
LOG2E = 1.4426950408889634
FAST_SPAN_LOG2 = 80.0


def _dot(a, b):
    return jnp.dot(a, b, preferred_element_type=F32)


def _dot_nt(a, b):
    return lax.dot_general(a, b, (((1,), (1,)), ((), ())), preferred_element_type=F32)


def _dot_tn(a, b):
    return lax.dot_general(a, b, (((0,), (0,)), ((), ())), preferred_element_type=F32)


def _split_bf16(a):
    hi = a.astype(BF16)
    lo = (a - hi.astype(F32)).astype(BF16)
    return hi, lo


def _gelu(a):
    return 0.5 * a * (1.0 + lax.erf(a * (1.0 / math.sqrt(2.0))))


def _sigmoid(a):
    return 0.5 * jnp.tanh(0.5 * a) + 0.5


def _silu(a):
    h = 0.5 * a
    return h + h * jnp.tanh(h)


def _rms(a):
    return a * lax.rsqrt(jnp.mean(a * a, axis=-1, keepdims=True) + EPS)


def _bcast_row_in_blocks(a, blk, row):
    m, n = a.shape
    a3 = a.reshape(m // blk, blk, n)
    return jnp.broadcast_to(a3[:, row:row + 1, :], (m // blk, blk, n)).reshape(m, n)


def _scores_single_reference(qs, k, b):
    c = qs.shape[0]
    ref_b = b[c // 2 - 1:c // 2, :]
    s = _dot_nt((qs * jnp.exp2(b - ref_b)).astype(BF16), (k * jnp.exp2(ref_b - b)).astype(BF16))
    causal = (lax.broadcasted_iota(jnp.int32, (c, c), 0) >= lax.broadcasted_iota(jnp.int32, (c, c), 1))
    return jnp.where(causal, s.astype(BF16), jnp.zeros((), BF16))


def _scores_hierarchical(qs, k, b):
    c = qs.shape[0]
    row = lax.broadcasted_iota(jnp.int32, (c, 1), 0)
    col = lax.broadcasted_iota(jnp.int32, (1, c), 1)
    scores = jnp.zeros((c, c), F32)
    m = SUBLANES
    while m < c:
        ref_b = _bcast_row_in_blocks(b, 2 * m, m - 1)
        e = jnp.exp2(-jnp.abs(b - ref_b))
        siblings = ((row // m) == (col // m) + 1) & ((col // m) % 2 == 0)
        scores = jnp.where(siblings, _dot_nt((qs * e).astype(BF16), (k * e).astype(BF16)), scores)
        m *= 2
    row_in = row % SUBLANES
    lane = lax.broadcasted_iota(jnp.int32, (1, LANES), 1)
    diag = jnp.zeros((c, LANES), F32)
    for j in range(SUBLANES):
        b_j = _bcast_row_in_blocks(b, SUBLANES, j)
        k_j = _bcast_row_in_blocks(k, SUBLANES, j)
        s_j = jnp.sum(qs * k_j * jnp.exp2(jnp.minimum(b - b_j, 0.0)), axis=-1, keepdims=True)
        diag = jnp.where((lane == j) & (row_in >= j), s_j, diag)
    expand = (lax.broadcasted_iota(jnp.int32, (LANES, c), 0)
              == lax.broadcasted_iota(jnp.int32, (LANES, c), 1) % SUBLANES).astype(BF16)
    diag_full = _dot(diag.astype(BF16), expand)
    same_block = (row // SUBLANES) == (col // SUBLANES)
    return jnp.where(same_block, diag_full.astype(BF16), scores.astype(BF16))


def _gla_inter_chunk(qs, k, v_b, b, state_ref, h):
    c = qs.shape[0]
    state = state_ref[h]
    o = _dot((qs * jnp.exp2(b)).astype(BF16), state.astype(BF16))
    b_last = b[c - 1:c, :]
    k_dec = (k * jnp.exp2(b_last - b)).astype(BF16)
    decay_col = jnp.transpose(jnp.broadcast_to(jnp.exp2(b_last), (LANES, B_HEAD_K)))[:, 0:1]
    state_ref[h] = decay_col * state + _dot_tn(k_dec, v_b)
    return o


def _stage_weights(w_in_t_hbm, sq_hbm, w_ada_hbm, c_ref, b_ada_ref, w_in_s, w_sq_s, mod_s, stage_ref, sem_ref):
    tasks = []

    bsz = c_ref.shape[0]
    act = _silu(c_ref[...])
    act = jnp.concatenate([act, jnp.zeros((SUBLANES - bsz, D_MODEL), F32)], axis=0)
    mod_s[...] = jnp.broadcast_to(b_ada_ref[...], mod_s.shape)
    for c0 in range(0, 3 * D_MODEL, D_MODEL):
        for r0 in range(0, D_MODEL, STAGE_ROWS):
            def consume(slot, c0=c0, r0=r0):
                a_hi, a_lo = _split_bf16(act[:, r0:r0 + STAGE_ROWS])
                w_hi, w_lo = _split_bf16(stage_ref[slot])
                mod_s[:, c0:c0 + D_MODEL] += _dot(a_hi, w_hi) + (_dot(a_lo, w_hi) + _dot(a_hi, w_lo))
            tasks.append((w_ada_hbm.at[0, pl.ds(r0, STAGE_ROWS), pl.ds(c0, D_MODEL)], STAGE_ROWS, consume))

    def in_task(src_row0, rows, dst_col0, valid):
        def consume(slot):
            blk = jnp.transpose(stage_ref[slot, 0:rows, :])
            if valid < rows:
                blk = jnp.where(lax.broadcasted_iota(jnp.int32, blk.shape, 1) < valid, blk, 0.0)
            w_in_s[:, dst_col0:dst_col0 + rows] = blk.astype(BF16)
        tasks.append((w_in_t_hbm.at[0, pl.ds(src_row0, rows), :], rows, consume))

    for c0 in range(0, W_MAIN, STAGE_ROWS):
        in_task(c0, STAGE_ROWS, c0, STAGE_ROWS)
    in_task(W_MAIN, LANES, OFF_G1, B_GATE_RANK)
    for c0 in range(0, 2 * D_MODEL, STAGE_ROWS):
        in_task(W_MAIN + B_GATE_RANK + c0, STAGE_ROWS, OFF_MA + c0, STAGE_ROWS)
    for m, w_hbm in enumerate(sq_hbm):
        for r0 in range(0, D_MODEL, STAGE_ROWS):
            def consume(slot, m=m, r0=r0):
                w_sq_s[m, r0:r0 + STAGE_ROWS, 0:D_MODEL] = stage_ref[slot].astype(BF16)
            tasks.append((w_hbm.at[0, pl.ds(r0, STAGE_ROWS), :], STAGE_ROWS, consume))

    copies = [pltpu.make_async_copy(src, stage_ref.at[i % STAGE_SLOTS, pl.ds(0, rows), :],
                                    sem_ref.at[i % STAGE_SLOTS])
              for i, (src, rows, _) in enumerate(tasks)]
    ahead = STAGE_SLOTS - 1
    for cp in copies[:ahead]:
        cp.start()
    for i, (_, _, consume) in enumerate(tasks):
        if i + ahead < len(tasks):
            copies[i + ahead].start()
        copies[i].wait()
        consume(i % STAGE_SLOTS)


def _block_kernel(x_ref, c_ref, w_ada_hbm, b_ada_ref, ng_ref, w_in_hbm, alng_ref, alnb_ref, aws_ref, abs_row_ref,
                  wg2_ref, bbg_ref, bng_ref, wpa_hbm, wpb_hbm, bm_ref, wout_hbm, fg_ref,
                  o_ref, state_ref, scores_ref, w_in_ref, w_sq_ref, mod_ref, abs_ref, hn_ref, stage_ref, sem_ref):
    @pl.when((pl.program_id(0) == 0) & (pl.program_id(1) == 0))
    def _():
        _stage_weights(w_in_hbm, (wpa_hbm, wpb_hbm, wout_hbm), w_ada_hbm, c_ref, b_ada_ref,
                       w_in_ref, w_sq_ref, mod_ref, stage_ref, sem_ref)
        for g in range(A_GROUPS):
            abs_ref[g] = jnp.transpose(jnp.broadcast_to(abs_row_ref[g:g + 1, :], (A_GROUP_DIM, A_CHUNK)))

    @pl.when(pl.program_id(1) == 0)
    def _():
        state_ref[...] = jnp.zeros_like(state_ref)

    tm = x_ref.shape[1]
    x = x_ref[0]
    mod = mod_ref[pl.ds(pl.program_id(0), 1), :]
    shift = mod[:, 0:D_MODEL]
    scale = mod[:, D_MODEL:2 * D_MODEL]
    gate = mod[:, 2 * D_MODEL:3 * D_MODEL]

    hn_ref[...] = (_rms(x) * (ng_ref[...] * (1.0 + scale)) + shift).astype(BF16)

    def proj(lo, width, w_ref=w_in_ref):
        return _dot(hn_ref[...], w_ref[:, lo:lo + width])

    g1 = proj(OFF_G1, LANES)
    q_all = proj(OFF_Q, B_KEY_DIM) * (B_HEAD_K ** -0.5)
    k_all = proj(OFF_K, B_KEY_DIM)
    wg2 = jnp.concatenate([wg2_ref[...], jnp.zeros((LANES - B_GATE_RANK, B_KEY_DIM), F32)], axis=0)
    gk = _dot(g1.astype(BF16), wg2.astype(BF16)) + bbg_ref[...]
    p_bv = proj(OFF_BV, B_VAL_DIM)
    log2_a = (jnp.minimum(gk, 0.0) - jnp.log1p(jnp.exp(-jnp.abs(gk)))) * (LOG2E / B_GATE_NORMALIZER)
    cc = GLA_CHUNK
    gla_chunks = tm // cc
    tril_b = (lax.broadcasted_iota(jnp.int32, (cc, cc), 0)
              >= lax.broadcasted_iota(jnp.int32, (cc, cc), 1)).astype(BF16)
    la_hi, la_lo = _split_bf16(log2_a)
    p_u = proj(OFF_U, D_MODEL)
    b_chunks = [_dot(tril_b, la_hi[ci * cc:(ci + 1) * cc]) + _dot(tril_b, la_lo[ci * cc:(ci + 1) * cc])
                for ci in range(gla_chunks)]
    v_all = p_bv.astype(BF16)

    head_args = []
    for ci in range(gla_chunks):
        rs = slice(ci * cc, (ci + 1) * cc)
        for h in range(B_HEADS):
            ks = slice(h * B_HEAD_K, (h + 1) * B_HEAD_K)
            head_args.append((q_all[rs, ks], k_all[rs, ks], b_chunks[ci][:, ks],
                              v_all[rs, h * B_HEAD_V:(h + 1) * B_HEAD_V]))

    p_v = proj(OFF_V, D_MODEL)
    o_inter = [_gla_inter_chunk(qs, k, v_b, b, state_ref, i % B_HEADS)
               for i, (qs, k, b, v_b) in enumerate(head_args)]

    p_z = proj(OFF_Z, D_MODEL)
    a_u = _gelu(p_u)
    a_v = _gelu(p_v)
    p_ma = proj(OFF_MA, D_MODEL)
    a_vc = a_v - jnp.mean(a_v, axis=-1, keepdims=True)
    a_vn = (a_vc * lax.rsqrt(jnp.mean(a_vc * a_vc, axis=-1, keepdims=True) + EPS)
            * alng_ref[...] + alnb_ref[...]).astype(BF16)

    tri = (lax.broadcasted_iota(jnp.int32, (A_CHUNK, A_CHUNK), 0)
           >= lax.broadcasted_iota(jnp.int32, (A_CHUNK, A_CHUNK), 1))
    n_chunks = tm // A_CHUNK
    mixed_g = []
    for g in range(A_GROUPS):
        cols = slice(g * A_GROUP_DIM, (g + 1) * A_GROUP_DIM)
        v_g = jnp.concatenate([a_vn[ci * A_CHUNK:(ci + 1) * A_CHUNK, cols] for ci in range(n_chunks)], axis=1)
        m_g = _dot(jnp.where(tri, aws_ref[g], 0.0).astype(BF16), v_g)
        mixed_g.append(jnp.concatenate(
            [m_g[:, ci * A_GROUP_DIM:(ci + 1) * A_GROUP_DIM] for ci in range(n_chunks)], axis=0) + jnp.concatenate(
            [abs_ref[g]] * n_chunks, axis=0))
    mixed = jnp.concatenate(mixed_g, axis=1)
    y_a = a_u * mixed * _silu(p_z)
    g_a = _sigmoid(p_ma + bm_ref[:, 0:D_MODEL])
    merged = g_a * _dot(y_a.astype(BF16), w_sq_ref[0, :, 0:D_MODEL])

    for i, (qs, k, b, _) in enumerate(head_args):
        scores_ref[i] = _scores_single_reference(qs, k, b)
    span = jnp.zeros((1, B_KEY_DIM), F32)
    for b in b_chunks:
        b_mid = b[cc // 2 - 1:cc // 2, :]
        span = jnp.maximum(span, jnp.maximum(-b_mid, b_mid - b[cc - 1:cc, :]))
    wide_span = jnp.max(span) >= FAST_SPAN_LOG2

    @pl.when(wide_span)
    def _():
        for i, (qs, k, b, _) in enumerate(head_args):
            scores_ref[i] = _scores_hierarchical(qs, k, b)

    p_bz = proj(OFF_BZ, B_VAL_DIM)
    p_mb = proj(OFF_MB, D_MODEL)
    for ci in range(gla_chunks):
        rs = slice(ci * cc, (ci + 1) * cc)
        heads = []
        for i in range(ci * B_HEADS, (ci + 1) * B_HEADS):
            o_h = o_inter[i] + _dot(scores_ref[i], head_args[i][3])
            heads.append(_rms(o_h) * bng_ref[...])
        o_b = jnp.concatenate(heads, axis=1) * _silu(p_bz[rs])
        y_b = _dot(o_b.astype(BF16), w_sq_ref[1, :, 0:D_MODEL])
        g_b = _sigmoid(p_mb[rs] + bm_ref[:, D_MODEL:2 * D_MODEL])
        merged_c = merged[rs] + g_b * y_b
        out = _dot(merged_c.astype(BF16), w_sq_ref[2, :, 0:D_MODEL])
        o_ref[0, rs, :] = _rms(x_ref[0, rs, :] + gate * out) * fg_ref[...]


def _const_spec(shape):
    nd = len(shape)
    return pl.BlockSpec(shape, lambda b, j: (0,) * nd, pipeline_mode=pl.Buffered(1))


def kernel(x, c, w_ada, b_ada, norm_g, w_in, a_ln_g, a_ln_b, a_w_s, a_b_s, b_w_gate2, b_b_gate, b_norm_g, w_proj_a, w_proj_b, b_merge, w_out, final_g):
    bsz, seq, d = x.shape
    assert d == D_MODEL and seq % TILE_M == 0 and w_ada.shape[0] == 1 and bsz <= SUBLANES
    l = 0

    row2 = lambda v: v.reshape(1, -1)
    in_hbm = {2, 5, 13, 14, 16}
    operands = (
        x, c, w_ada, b_ada, row2(norm_g[l]), jnp.swapaxes(w_in, 1, 2), row2(a_ln_g[l]), row2(a_ln_b[l]),
        a_w_s[l], a_b_s[l], b_w_gate2[l], row2(b_b_gate[l]), row2(b_norm_g[l]), w_proj_a, w_proj_b,
        row2(b_merge[l]), w_out, row2(final_g),
    )
    in_specs = [pl.BlockSpec((1, TILE_M, d), lambda b, j: (b, j, 0))] + [
        pl.BlockSpec(memory_space=pl.ANY) if i in in_hbm else _const_spec(op.shape)
        for i, op in enumerate(operands) if i >= 1]

    return pl.pallas_call(
        _block_kernel,
        grid=(bsz, seq // TILE_M),
        in_specs=in_specs,
        out_specs=pl.BlockSpec((1, TILE_M, d), lambda b, j: (b, j, 0)),
        out_shape=jax.ShapeDtypeStruct((bsz, seq, d), x.dtype),
        scratch_shapes=[pltpu.VMEM((B_HEADS, B_HEAD_K, B_HEAD_V), F32),
                        pltpu.VMEM((TILE_M // GLA_CHUNK * B_HEADS, GLA_CHUNK, GLA_CHUNK), BF16),
                        pltpu.VMEM((d, W_IN_RESIDENT), BF16),
                        pltpu.VMEM((3, d, W_SQ_RESIDENT), BF16),
                        pltpu.VMEM((SUBLANES, 3 * d), F32),
                        pltpu.VMEM((A_GROUPS, A_CHUNK, A_GROUP_DIM), F32),
                        pltpu.VMEM((TILE_M, d), BF16),
                        pltpu.VMEM((STAGE_SLOTS, STAGE_ROWS, D_MODEL), F32),
                        pltpu.SemaphoreType.DMA((STAGE_SLOTS,))],
        compiler_params=pltpu.CompilerParams(
            dimension_semantics=("arbitrary", "arbitrary"),
            vmem_limit_bytes=VMEM_LIMIT_BYTES),
        name="hybrid_block",
    )(*operands)
```

```python
import math

import jax
import jax.numpy as jnp
from jax import lax
from jax.experimental import pallas as pl
from jax.experimental.pallas import tpu as pltpu

D_MODEL = 1024
A_GROUPS = 8
A_CHUNK = 128
A_GROUP_DIM = D_MODEL // A_GROUPS
B_HEADS = 4
B_KEY_DIM = D_MODEL // 2
B_VAL_DIM = D_MODEL
B_HEAD_K = B_KEY_DIM // B_HEADS
B_HEAD_V = B_VAL_DIM // B_HEADS
B_GATE_RANK = 16
B_GATE_NORMALIZER = 16.0
EPS = 1e-6

LANES = 128
SUBLANES = 8
GLA_CHUNK = 256
TILE_M = 512
VMEM_LIMIT_BYTES = 60 * 1024 * 1024

OFF_U, OFF_V, OFF_Z = 0, 1024, 2048
OFF_Q, OFF_K = 3072, 3584
OFF_BV, OFF_BZ = 4096, 5120
W_MAIN = 6 * D_MODEL
IN_WIDTH = W_MAIN + B_GATE_RANK + 2 * D_MODEL
OFF_G1 = W_MAIN
OFF_MA, OFF_MB = W_MAIN + LANES, W_MAIN + LANES + D_MODEL
W_IN_RESIDENT = OFF_MB + D_MODEL
W_SQ_RESIDENT = D_MODEL + LANES
STAGE_ROWS = 128
STAGE_SLOTS = 8

F32 = jnp.float32
BF16 = jnp.bfloat16
LOG2E = 1.4426950408889634
FAST_SPAN_LOG2 = 80.0


def _dot(a, b):
    return jnp.dot(a, b, preferred_element_type=F32)


def _dot_nt(a, b):
    return lax.dot_general(a, b, (((1,), (1,)), ((), ())), preferred_element_type=F32)


def _dot_tn(a, b):
    return lax.dot_general(a, b, (((0,), (0,)), ((), ())), preferred_element_type=F32)


def _split_bf16(a):
    hi = a.astype(BF16)
    lo = (a - hi.astype(F32)).astype(BF16)
    return hi, lo


def _gelu(a):
    return 0.5 * a * (1.0 + lax.erf(a * (1.0 / math.sqrt(2.0))))


def _sigmoid(a):
    return 0.5 * jnp.tanh(0.5 * a) + 0.5


def _silu(a):
    h = 0.5 * a
    return h + h * jnp.tanh(h)


def _rms(a):
    return a * lax.rsqrt(jnp.mean(a * a, axis=-1, keepdims=True) + EPS)


def _bcast_row_in_blocks(a, blk, row):
    m, n = a.shape
    a3 = a.reshape(m // blk, blk, n)
    return jnp.broadcast_to(a3[:, row:row + 1, :], (m // blk, blk, n)).reshape(m, n)


def _scores_single_reference(qs, k, b):
    c = qs.shape[0]
    ref_b = b[c // 2 - 1:c // 2, :]
    s = _dot_nt((qs * jnp.exp2(b - ref_b)).astype(BF16), (k * jnp.exp2(ref_b - b)).astype(BF16))
    causal = (lax.broadcasted_iota(jnp.int32, (c, c), 0) >= lax.broadcasted_iota(jnp.int32, (c, c), 1))
    return jnp.where(causal, s.astype(BF16), jnp.zeros((), BF16))


def _scores_hierarchical(qs, k, b):
    c = qs.shape[0]
    row = lax.broadcasted_iota(jnp.int32, (c, 1), 0)
    col = lax.broadcasted_iota(jnp.int32, (1, c), 1)
    scores = jnp.zeros((c, c), F32)
    m = SUBLANES
    while m < c:
        ref_b = _bcast_row_in_blocks(b, 2 * m, m - 1)
        e = jnp.exp2(-jnp.abs(b - ref_b))
        siblings = ((row // m) == (col // m) + 1) & ((col // m) % 2 == 0)
        scores = jnp.where(siblings, _dot_nt((qs * e).astype(BF16), (k * e).astype(BF16)), scores)
        m *= 2
    row_in = row % SUBLANES
    lane = lax.broadcasted_iota(jnp.int32, (1, LANES), 1)
    diag = jnp.zeros((c, LANES), F32)
    for j in range(SUBLANES):
        b_j = _bcast_row_in_blocks(b, SUBLANES, j)
        k_j = _bcast_row_in_blocks(k, SUBLANES, j)
        s_j = jnp.sum(qs * k_j * jnp.exp2(jnp.minimum(b - b_j, 0.0)), axis=-1, keepdims=True)
        diag = jnp.where((lane == j) & (row_in >= j), s_j, diag)
    expand = (lax.broadcasted_iota(jnp.int32, (LANES, c), 0)
              == lax.broadcasted_iota(jnp.int32, (LANES, c), 1) % SUBLANES).astype(BF16)
    diag_full = _dot(diag.astype(BF16), expand)
    same_block = (row // SUBLANES) == (col // SUBLANES)
    return jnp.where(same_block, diag_full.astype(BF16), scores.astype(BF16))


def _gla_inter_chunk(qs, k, v_b, b, state_ref, h):
    c = qs.shape[0]
    state = state_ref[h]
    o = _dot((qs * jnp.exp2(b)).astype(BF16), state.astype(BF16))
    b_last = b[c - 1:c, :]
    k_dec = (k * jnp.exp2(b_last - b)).astype(BF16)
    decay_col = jnp.transpose(jnp.broadcast_to(jnp.exp2(b_last), (LANES, B_HEAD_K)))[:, 0:1]
    state_ref[h] = decay_col * state + _dot_tn(k_dec, v_b)
    return o


def _stage_weights(w_in_t_hbm, sq_hbm, w_ada_hbm, c_ref, b_ada_ref, w_in_s, w_sq_s, mod_s, stage_ref, sem_ref):
    tasks = []

    bsz = c_ref.shape[0]
    act = _silu(c_ref[...])
    act = jnp.concatenate([act, jnp.zeros((SUBLANES - bsz, D_MODEL), F32)], axis=0)
    mod_s[...] = jnp.broadcast_to(b_ada_ref[...], mod_s.shape)
    for c0 in range(0, 3 * D_MODEL, D_MODEL):
        for r0 in range(0, D_MODEL, STAGE_ROWS):
            def consume(slot, c0=c0, r0=r0):
                a_hi, a_lo = _split_bf16(act[:, r0:r0 + STAGE_ROWS])
                w_hi, w_lo = _split_bf16(stage_ref[slot])
                mod_s[:, c0:c0 + D_MODEL] += _dot(a_hi, w_hi) + (_dot(a_lo, w_hi) + _dot(a_hi, w_lo))
            tasks.append((w_ada_hbm.at[0, pl.ds(r0, STAGE_ROWS), pl.ds(c0, D_MODEL)], STAGE_ROWS, consume))

    def in_task(src_row0, rows, dst_col0, valid):
        def consume(slot):
            blk = jnp.transpose(stage_ref[slot, 0:rows, :])
            if valid < rows:
                blk = jnp.where(lax.broadcasted_iota(jnp.int32, blk.shape, 1) < valid, blk, 0.0)
            w_in_s[:, dst_col0:dst_col0 + rows] = blk.astype(BF16)
        tasks.append((w_in_t_hbm.at[0, pl.ds(src_row0, rows), :], rows, consume))

    for c0 in range(0, W_MAIN, STAGE_ROWS):
        in_task(c0, STAGE_ROWS, c0, STAGE_ROWS)
    in_task(W_MAIN, LANES, OFF_G1, B_GATE_RANK)
    for c0 in range(0, 2 * D_MODEL, STAGE_ROWS):
        in_task(W_MAIN + B_GATE_RANK + c0, STAGE_ROWS, OFF_MA + c0, STAGE_ROWS)
    for m, w_hbm in enumerate(sq_hbm):
        for r0 in range(0, D_MODEL, STAGE_ROWS):
            def consume(slot, m=m, r0=r0):
                w_sq_s[m, r0:r0 + STAGE_ROWS, 0:D_MODEL] = stage_ref[slot].astype(BF16)
            tasks.append((w_hbm.at[0, pl.ds(r0, STAGE_ROWS), :], STAGE_ROWS, consume))

    copies = [pltpu.make_async_copy(src, stage_ref.at[i % STAGE_SLOTS, pl.ds(0, rows), :],
                                    sem_ref.at[i % STAGE_SLOTS])
              for i, (src, rows, _) in enumerate(tasks)]
    ahead = STAGE_SLOTS - 1
    for cp in copies[:ahead]:
        cp.start()
    for i, (_, _, consume) in enumerate(tasks):
        if i + ahead < len(tasks):
            copies[i + ahead].start()
        copies[i].wait()
        consume(i % STAGE_SLOTS)


def _block_kernel(x_ref, c_ref, w_ada_hbm, b_ada_ref, ng_ref, w_in_hbm, alng_ref, alnb_ref, aws_ref, abs_row_ref,
                  wg2_ref, bbg_ref, bng_ref, wpa_hbm, wpb_hbm, bm_ref, wout_hbm, fg_ref,
                  o_ref, state_ref, scores_ref, w_in_ref, w_sq_ref, mod_ref, abs_ref, hn_ref, stage_ref, sem_ref):
    @pl.when((pl.program_id(0) == 0) & (pl.program_id(1) == 0))
    def _():
        _stage_weights(w_in_hbm, (wpa_hbm, wpb_hbm, wout_hbm), w_ada_hbm, c_ref, b_ada_ref,
                       w_in_ref, w_sq_ref, mod_ref, stage_ref, sem_ref)
        for g in range(A_GROUPS):
            abs_ref[g] = jnp.transpose(jnp.broadcast_to(abs_row_ref[g:g + 1, :], (A_GROUP_DIM, A_CHUNK)))

    @pl.when(pl.program_id(1) == 0)
    def _():
        state_ref[...] = jnp.zeros_like(state_ref)

    tm = x_ref.shape[1]
    x = x_ref[0]
    mod = mod_ref[pl.ds(pl.program_id(0), 1), :]
    shift = mod[:, 0:D_MODEL]
    scale = mod[:, D_MODEL:2 * D_MODEL]
    gate = mod[:, 2 * D_MODEL:3 * D_MODEL]

    hn_ref[...] = (_rms(x) * (ng_ref[...] * (1.0 + scale)) + shift).astype(BF16)

    def proj(lo, width, w_ref=w_in_ref):
        return _dot(hn_ref[...], w_ref[:, lo:lo + width])

    g1 = proj(OFF_G1, LANES)
    q_all = proj(OFF_Q, B_KEY_DIM) * (B_HEAD_K ** -0.5)
    k_all = proj(OFF_K, B_KEY_DIM)
    wg2 = jnp.concatenate([wg2_ref[...], jnp.zeros((LANES - B_GATE_RANK, B_KEY_DIM), F32)], axis=0)
    gk = _dot(g1.astype(BF16), wg2.astype(BF16)) + bbg_ref[...]
    p_bv = proj(OFF_BV, B_VAL_DIM)
    log2_a = (jnp.minimum(gk, 0.0) - jnp.log1p(jnp.exp(-jnp.abs(gk)))) * (LOG2E / B_GATE_NORMALIZER)
    cc = GLA_CHUNK
    gla_chunks = tm // cc
    tril_b = (lax.broadcasted_iota(jnp.int32, (cc, cc), 0)
              >= lax.broadcasted_iota(jnp.int32, (cc, cc), 1)).astype(BF16)
    la_hi, la_lo = _split_bf16(log2_a)
    p_u = proj(OFF_U, D_MODEL)
    b_chunks = [_dot(tril_b, la_hi[ci * cc:(ci + 1) * cc]) + _dot(tril_b, la_lo[ci * cc:(ci + 1) * cc])
                for ci in range(gla_chunks)]
    v_all = p_bv.astype(BF16)

    head_args = []
    for ci in range(gla_chunks):
        rs = slice(ci * cc, (ci + 1) * cc)
        for h in range(B_HEADS):
            ks = slice(h * B_HEAD_K, (h + 1) * B_HEAD_K)
            head_args.append((q_all[rs, ks], k_all[rs, ks], b_chunks[ci][:, ks],
                              v_all[rs, h * B_HEAD_V:(h + 1) * B_HEAD_V]))

    p_v = proj(OFF_V, D_MODEL)
    o_inter = [_gla_inter_chunk(qs, k, v_b, b, state_ref, i % B_HEADS)
               for i, (qs, k, b, v_b) in enumerate(head_args)]

    p_z = proj(OFF_Z, D_MODEL)
    a_u = _gelu(p_u)
    a_v = _gelu(p_v)
    p_ma = proj(OFF_MA, D_MODEL)
    a_vc = a_v - jnp.mean(a_v, axis=-1, keepdims=True)
    a_vn = (a_vc * lax.rsqrt(jnp.mean(a_vc * a_vc, axis=-1, keepdims=True) + EPS)
            * alng_ref[...] + alnb_ref[...]).astype(BF16)

    tri = (lax.broadcasted_iota(jnp.int32, (A_CHUNK, A_CHUNK), 0)
           >= lax.broadcasted_iota(jnp.int32, (A_CHUNK, A_CHUNK), 1))
    n_chunks = tm // A_CHUNK
    mixed_g = []
    for g in range(A_GROUPS):
        cols = slice(g * A_GROUP_DIM, (g + 1) * A_GROUP_DIM)
        v_g = jnp.concatenate([a_vn[ci * A_CHUNK:(ci + 1) * A_CHUNK, cols] for ci in range(n_chunks)], axis=1)
        m_g = _dot(jnp.where(tri, aws_ref[g], 0.0).astype(BF16), v_g)
        mixed_g.append(jnp.concatenate(
            [m_g[:, ci * A_GROUP_DIM:(ci + 1) * A_GROUP_DIM] for ci in range(n_chunks)], axis=0) + jnp.concatenate(
            [abs_ref[g]] * n_chunks, axis=0))
    mixed = jnp.concatenate(mixed_g, axis=1)
    y_a = a_u * mixed * _silu(p_z)
    g_a = _sigmoid(p_ma + bm_ref[:, 0:D_MODEL])
    merged = g_a * _dot(y_a.astype(BF16), w_sq_ref[0, :, 0:D_MODEL])

    for i, (qs, k, b, _) in enumerate(head_args):
        scores_ref[i] = _scores_single_reference(qs, k, b)
    span = jnp.zeros((1, B_KEY_DIM), F32)
    for b in b_chunks:
        b_mid = b[cc // 2 - 1:cc // 2, :]
        span = jnp.maximum(span, jnp.maximum(-b_mid, b_mid - b[cc - 1:cc, :]))
    wide_span = jnp.max(span) >= FAST_SPAN_LOG2

    @pl.when(wide_span)
    def _():
        for i, (qs, k, b, _) in enumerate(head_args):
            scores_ref[i] = _scores_hierarchical(qs, k, b)

    o_attn = []
    for ci in range(gla_chunks):
        heads = []
        for i in range(ci * B_HEADS, (ci + 1) * B_HEADS):
            o_h = o_inter[i] + _dot(scores_ref[i], head_args[i][3])
            heads.append(_rms(o_h) * bng_ref[...])
        o_attn.append(jnp.concatenate(heads, axis=1))
    p_bz = proj(OFF_BZ, B_VAL_DIM)
    p_mb = proj(OFF_MB, D_MODEL)
    for ci in range(gla_chunks):
        rs = slice(ci * cc, (ci + 1) * cc)
        o_b = o_attn[ci] * _silu(p_bz[rs])
        y_b = _dot(o_b.astype(BF16), w_sq_ref[1, :, 0:D_MODEL])
        g_b = _sigmoid(p_mb[rs] + bm_ref[:, D_MODEL:2 * D_MODEL])
        merged_c = merged[rs] + g_b * y_b
        out = _dot(merged_c.astype(BF16), w_sq_ref[2, :, 0:D_MODEL])
        o_ref[0, rs, :] = _rms(x_ref[0, rs, :] + gate * out) * fg_ref[...]


def _const_spec(shape):
    nd = len(shape)
    return pl.BlockSpec(shape, lambda b, j: (0,) * nd, pipeline_mode=pl.Buffered(1))


def kernel(x, c, w_ada, b_ada, norm_g, w_in, a_ln_g, a_ln_b, a_w_s, a_b_s, b_w_gate2, b_b_gate, b_norm_g, w_proj_a, w_proj_b, b_merge, w_out, final_g):
    bsz, seq, d = x.shape
    assert d == D_MODEL and seq % TILE_M == 0 and w_ada.shape[0] == 1 and bsz <= SUBLANES
    l = 0

    row2 = lambda v: v.reshape(1, -1)
    in_hbm = {2, 5, 13, 14, 16}
    operands = (
        x, c, w_ada, b_ada, row2(norm_g[l]), jnp.swapaxes(w_in, 1, 2), row2(a_ln_g[l]), row2(a_ln_b[l]),
        a_w_s[l], a_b_s[l], b_w_gate2[l], row2(b_b_gate[l]), row2(b_norm_g[l]), w_proj_a, w_proj_b,
        row2(b_merge[l]), w_out, row2(final_g),
    )
    in_specs = [pl.BlockSpec((1, TILE_M, d), lambda b, j: (b, j, 0))] + [
        pl.BlockSpec(memory_space=pl.ANY) if i in in_hbm else _const_spec(op.shape)
        for i, op in enumerate(operands) if i >= 1]

    return pl.pallas_call(
        _block_kernel,
        grid=(bsz, seq // TILE_M),
        in_specs=in_specs,
        out_specs=pl.BlockSpec((1, TILE_M, d), lambda b, j: (b, j, 0)),
        out_shape=jax.ShapeDtypeStruct((bsz, seq, d), x.dtype),
        scratch_shapes=[pltpu.VMEM((B_HEADS, B_HEAD_K, B_HEAD_V), F32),
                        pltpu.VMEM((TILE_M // GLA_CHUNK * B_HEADS, GLA_CHUNK, GLA_CHUNK), BF16),
                        pltpu.VMEM((d, W_IN_RESIDENT), BF16),
                        pltpu.VMEM((3, d, W_SQ_RESIDENT), BF16),
                        pltpu.VMEM((SUBLANES, 3 * d), F32),
                        pltpu.VMEM((A_GROUPS, A_CHUNK, A_GROUP_DIM), F32),
                        pltpu.VMEM((TILE_M, d), BF16),
                        pltpu.VMEM((STAGE_SLOTS, STAGE_ROWS, D_MODEL), F32),
                        pltpu.SemaphoreType.DMA((STAGE_SLOTS,))],
        compiler_params=pltpu.CompilerParams(
            dimension_semantics=("arbitrary", "arbitrary"),
            vmem_limit_bytes=VMEM_LIMIT_BYTES),
        name="hybrid_block",
    )(*operands)
```

```python
import math

import jax
import jax.numpy as jnp
from jax import lax
from jax.experimental import pallas as pl
from jax.experimental.pallas import tpu as pltpu

D_MODEL = 1024
A_GROUPS = 8
A_CHUNK = 128
A_GROUP_DIM = D_MODEL // A_GROUPS
B_HEADS = 4
B_KEY_DIM = D_MODEL // 2
B_VAL_DIM = D_MODEL
B_HEAD_K = B_KEY_DIM // B_HEADS
B_HEAD_V = B_VAL_DIM // B_HEADS
B_GATE_RANK = 16
B_GATE_NORMALIZER = 16.0
EPS = 1e-6

LANES = 128
SUBLANES = 8
GLA_CHUNK = 256
TILE_M = 512
VMEM_LIMIT_BYTES = 60 * 1024 * 1024

OFF_U, OFF_V, OFF_Z = 0, 1024, 2048
OFF_Q, OFF_K = 3072, 3584
OFF_BV, OFF_BZ = 4096, 5120
W_MAIN = 6 * D_MODEL
IN_WIDTH = W_MAIN + B_GATE_RANK + 2 * D_MODEL
OFF_G1 = W_MAIN
OFF_MA, OFF_MB = W_MAIN + LANES, W_MAIN + LANES + D_MODEL
W_IN_RESIDENT = OFF_MB + D_MODEL
W_SQ_RESIDENT = D_MODEL + LANES
STAGE_ROWS = 128
STAGE_SLOTS = 8

F32 = jnp.float32
BF16 = jnp.bfloat16
LOG2E = 1.4426950408889634
FAST_SPAN_LOG2 = 80.0


def _dot(a, b):
    return jnp.dot(a, b, preferred_element_type=F32)


def _dot_nt(a, b):
    return lax.dot_general(a, b, (((1,), (1,)), ((), ())), preferred_element_type=F32)


def _dot_tn(a, b):
    return lax.dot_general(a, b, (((0,), (0,)), ((), ())), preferred_element_type=F32)


def _split_bf16(a):
    hi = a.astype(BF16)
    lo = (a - hi.astype(F32)).astype(BF16)
    return hi, lo


def _gelu(a):
    return 0.5 * a * (1.0 + lax.erf(a * (1.0 / math.sqrt(2.0))))


def _sigmoid(a):
    return 0.5 * jnp.tanh(0.5 * a) + 0.5


def _silu(a):
    h = 0.5 * a
    return h + h * jnp.tanh(h)


def _rms(a):
    return a * lax.rsqrt(jnp.mean(a * a, axis=-1, keepdims=True) + EPS)


def _bcast_row_in_blocks(a, blk, row):
    m, n = a.shape
    a3 = a.reshape(m // blk, blk, n)
    return jnp.broadcast_to(a3[:, row:row + 1, :], (m // blk, blk, n)).reshape(m, n)


def _scores_single_reference(qs, k, b):
    c = qs.shape[0]
    ref_b = b[c // 2 - 1:c // 2, :]
    s = _dot_nt((qs * jnp.exp2(b - ref_b)).astype(BF16), (k * jnp.exp2(ref_b - b)).astype(BF16))
    causal = (lax.broadcasted_iota(jnp.int32, (c, c), 0) >= lax.broadcasted_iota(jnp.int32, (c, c), 1))
    return jnp.where(causal, s.astype(BF16), jnp.zeros((), BF16))


def _scores_hierarchical(qs, k, b):
    c = qs.shape[0]
    row = lax.broadcasted_iota(jnp.int32, (c, 1), 0)
    col = lax.broadcasted_iota(jnp.int32, (1, c), 1)
    scores = jnp.zeros((c, c), F32)
    m = SUBLANES
    while m < c:
        ref_b = _bcast_row_in_blocks(b, 2 * m, m - 1)
        e = jnp.exp2(-jnp.abs(b - ref_b))
        siblings = ((row // m) == (col // m) + 1) & ((col // m) % 2 == 0)
        scores = jnp.where(siblings, _dot_nt((qs * e).astype(BF16), (k * e).astype(BF16)), scores)
        m *= 2
    row_in = row % SUBLANES
    lane = lax.broadcasted_iota(jnp.int32, (1, LANES), 1)
    diag = jnp.zeros((c, LANES), F32)
    for j in range(SUBLANES):
        b_j = _bcast_row_in_blocks(b, SUBLANES, j)
        k_j = _bcast_row_in_blocks(k, SUBLANES, j)
        s_j = jnp.sum(qs * k_j * jnp.exp2(jnp.minimum(b - b_j, 0.0)), axis=-1, keepdims=True)
        diag = jnp.where((lane == j) & (row_in >= j), s_j, diag)
    expand = (lax.broadcasted_iota(jnp.int32, (LANES, c), 0)
              == lax.broadcasted_iota(jnp.int32, (LANES, c), 1) % SUBLANES).astype(BF16)
    diag_full = _dot(diag.astype(BF16), expand)
    same_block = (row // SUBLANES) == (col // SUBLANES)
    return jnp.where(same_block, diag_full.astype(BF16), scores.astype(BF16))


def _gla_inter_chunk(qs, k, v_b, b, state_ref, h):
    c = qs.shape[0]
    state = state_ref[h]
    o = _dot((qs * jnp.exp2(b)).astype(BF16), state.astype(BF16))
    b_last = b[c - 1:c, :]
    k_dec = (k * jnp.exp2(b_last - b)).astype(BF16)
    decay_col = jnp.transpose(jnp.broadcast_to(jnp.exp2(b_last), (LANES, B_HEAD_K)))[:, 0:1]
    state_ref[h] = decay_col * state + _dot_tn(k_dec, v_b)
    return o


def _stage_weights(w_in_t_hbm, sq_hbm, w_ada_hbm, c_ref, b_ada_ref, w_in_s, w_sq_s, mod_s, stage_ref, sem_ref):
    tasks = []

    bsz = c_ref.shape[0]
    act = _silu(c_ref[...])
    act = jnp.concatenate([act, jnp.zeros((SUBLANES - bsz, D_MODEL), F32)], axis=0)
    mod_s[...] = jnp.broadcast_to(b_ada_ref[...], mod_s.shape)
    for c0 in range(0, 3 * D_MODEL, D_MODEL):
        for r0 in range(0, D_MODEL, STAGE_ROWS):
            def consume(slot, c0=c0, r0=r0):
                a_hi, a_lo = _split_bf16(act[:, r0:r0 + STAGE_ROWS])
                w_hi, w_lo = _split_bf16(stage_ref[slot])
                mod_s[:, c0:c0 + D_MODEL] += _dot(a_hi, w_hi) + (_dot(a_lo, w_hi) + _dot(a_hi, w_lo))
            tasks.append((w_ada_hbm.at[0, pl.ds(r0, STAGE_ROWS), pl.ds(c0, D_MODEL)], STAGE_ROWS, consume))

    def in_task(src_row0, rows, dst_col0, valid):
        def consume(slot):
            blk = jnp.transpose(stage_ref[slot, 0:rows, :])
            if valid < rows:
                blk = jnp.where(lax.broadcasted_iota(jnp.int32, blk.shape, 1) < valid, blk, 0.0)
            w_in_s[:, dst_col0:dst_col0 + rows] = blk.astype(BF16)
        tasks.append((w_in_t_hbm.at[0, pl.ds(src_row0, rows), :], rows, consume))

    for c0 in range(0, W_MAIN, STAGE_ROWS):
        in_task(c0, STAGE_ROWS, c0, STAGE_ROWS)
    in_task(W_MAIN, LANES, OFF_G1, B_GATE_RANK)
    for c0 in range(0, 2 * D_MODEL, STAGE_ROWS):
        in_task(W_MAIN + B_GATE_RANK + c0, STAGE_ROWS, OFF_MA + c0, STAGE_ROWS)
    for m, w_hbm in enumerate(sq_hbm):
        for r0 in range(0, D_MODEL, STAGE_ROWS):
            def consume(slot, m=m, r0=r0):
                w_sq_s[m, r0:r0 + STAGE_ROWS, 0:D_MODEL] = stage_ref[slot].astype(BF16)
            tasks.append((w_hbm.at[0, pl.ds(r0, STAGE_ROWS), :], STAGE_ROWS, consume))

    copies = [pltpu.make_async_copy(src, stage_ref.at[i % STAGE_SLOTS, pl.ds(0, rows), :],
                                    sem_ref.at[i % STAGE_SLOTS])
              for i, (src, rows, _) in enumerate(tasks)]
    ahead = STAGE_SLOTS - 1
    for i, cp in enumerate(copies[:ahead]):
        cp.start(priority=i % 2)
    for i, (_, _, consume) in enumerate(tasks):
        if i + ahead < len(tasks):
            copies[i + ahead].start(priority=(i + ahead) % 2)
        copies[i].wait()
        consume(i % STAGE_SLOTS)


def _block_kernel(x_ref, c_ref, w_ada_hbm, b_ada_ref, ng_ref, w_in_hbm, alng_ref, alnb_ref, aws_ref, abs_row_ref,
                  wg2_ref, bbg_ref, bng_ref, wpa_hbm, wpb_hbm, bm_ref, wout_hbm, fg_ref,
                  o_ref, state_ref, scores_ref, w_in_ref, w_sq_ref, mod_ref, abs_ref, hn_ref, stage_ref, sem_ref):
    @pl.when((pl.program_id(0) == 0) & (pl.program_id(1) == 0))
    def _():
        _stage_weights(w_in_hbm, (wpa_hbm, wpb_hbm, wout_hbm), w_ada_hbm, c_ref, b_ada_ref,
                       w_in_ref, w_sq_ref, mod_ref, stage_ref, sem_ref)
        for g in range(A_GROUPS):
            abs_ref[g] = jnp.transpose(jnp.broadcast_to(abs_row_ref[g:g + 1, :], (A_GROUP_DIM, A_CHUNK)))

    @pl.when(pl.program_id(1) == 0)
    def _():
        state_ref[...] = jnp.zeros_like(state_ref)

    tm = x_ref.shape[1]
    x = x_ref[0]
    mod = mod_ref[pl.ds(pl.program_id(0), 1), :]
    shift = mod[:, 0:D_MODEL]
    scale = mod[:, D_MODEL:2 * D_MODEL]
    gate = mod[:, 2 * D_MODEL:3 * D_MODEL]

    hn_ref[...] = (_rms(x) * (ng_ref[...] * (1.0 + scale)) + shift).astype(BF16)

    def proj(lo, width, w_ref=w_in_ref):
        return _dot(hn_ref[...], w_ref[:, lo:lo + width])

    g1 = proj(OFF_G1, LANES)
    q_all = proj(OFF_Q, B_KEY_DIM) * (B_HEAD_K ** -0.5)
    k_all = proj(OFF_K, B_KEY_DIM)
    wg2 = jnp.concatenate([wg2_ref[...], jnp.zeros((LANES - B_GATE_RANK, B_KEY_DIM), F32)], axis=0)
    gk = _dot(g1.astype(BF16), wg2.astype(BF16)) + bbg_ref[...]
    p_bv = proj(OFF_BV, B_VAL_DIM)
    log2_a = (jnp.minimum(gk, 0.0) - jnp.log1p(jnp.exp(-jnp.abs(gk)))) * (LOG2E / B_GATE_NORMALIZER)
    cc = GLA_CHUNK
    gla_chunks = tm // cc
    tril_b = (lax.broadcasted_iota(jnp.int32, (cc, cc), 0)
              >= lax.broadcasted_iota(jnp.int32, (cc, cc), 1)).astype(BF16)
    la_hi, la_lo = _split_bf16(log2_a)
    p_u = proj(OFF_U, D_MODEL)
    b_chunks = [_dot(tril_b, la_hi[ci * cc:(ci + 1) * cc]) + _dot(tril_b, la_lo[ci * cc:(ci + 1) * cc])
                for ci in range(gla_chunks)]
    v_all = p_bv.astype(BF16)

    head_args = []
    for ci in range(gla_chunks):
        rs = slice(ci * cc, (ci + 1) * cc)
        for h in range(B_HEADS):
            ks = slice(h * B_HEAD_K, (h + 1) * B_HEAD_K)
            head_args.append((q_all[rs, ks], k_all[rs, ks], b_chunks[ci][:, ks],
                              v_all[rs, h * B_HEAD_V:(h + 1) * B_HEAD_V]))

    p_v = proj(OFF_V, D_MODEL)
    o_inter = [_gla_inter_chunk(qs, k, v_b, b, state_ref, i % B_HEADS)
               for i, (qs, k, b, v_b) in enumerate(head_args)]

    p_z = proj(OFF_Z, D_MODEL)
    a_u = _gelu(p_u)
    a_v = _gelu(p_v)
    p_ma = proj(OFF_MA, D_MODEL)
    a_vc = a_v - jnp.mean(a_v, axis=-1, keepdims=True)
    a_vn = (a_vc * lax.rsqrt(jnp.mean(a_vc * a_vc, axis=-1, keepdims=True) + EPS)
            * alng_ref[...] + alnb_ref[...]).astype(BF16)

    tri = (lax.broadcasted_iota(jnp.int32, (A_CHUNK, A_CHUNK), 0)
           >= lax.broadcasted_iota(jnp.int32, (A_CHUNK, A_CHUNK), 1))
    n_chunks = tm // A_CHUNK
    mixed_g = []
    for g in range(A_GROUPS):
        cols = slice(g * A_GROUP_DIM, (g + 1) * A_GROUP_DIM)
        v_g = jnp.concatenate([a_vn[ci * A_CHUNK:(ci + 1) * A_CHUNK, cols] for ci in range(n_chunks)], axis=1)
        m_g = _dot(jnp.where(tri, aws_ref[g], 0.0).astype(BF16), v_g)
        mixed_g.append(jnp.concatenate(
            [m_g[:, ci * A_GROUP_DIM:(ci + 1) * A_GROUP_DIM] for ci in range(n_chunks)], axis=0) + jnp.concatenate(
            [abs_ref[g]] * n_chunks, axis=0))
    mixed = jnp.concatenate(mixed_g, axis=1)
    y_a = a_u * mixed * _silu(p_z)
    g_a = _sigmoid(p_ma + bm_ref[:, 0:D_MODEL])
    merged = g_a * _dot(y_a.astype(BF16), w_sq_ref[0, :, 0:D_MODEL])

    for i, (qs, k, b, _) in enumerate(head_args):
        scores_ref[i] = _scores_single_reference(qs, k, b)
    span = jnp.zeros((1, B_KEY_DIM), F32)
    for b in b_chunks:
        b_mid = b[cc // 2 - 1:cc // 2, :]
        span = jnp.maximum(span, jnp.maximum(-b_mid, b_mid - b[cc - 1:cc, :]))
    wide_span = jnp.max(span) >= FAST_SPAN_LOG2

    @pl.when(wide_span)
    def _():
        for i, (qs, k, b, _) in enumerate(head_args):
            scores_ref[i] = _scores_hierarchical(qs, k, b)

    o_attn = []
    for ci in range(gla_chunks):
        heads = []
        for i in range(ci * B_HEADS, (ci + 1) * B_HEADS):
            o_h = o_inter[i] + _dot(scores_ref[i], head_args[i][3])
            heads.append(_rms(o_h) * bng_ref[...])
        o_attn.append(jnp.concatenate(heads, axis=1))
    p_bz = proj(OFF_BZ, B_VAL_DIM)
    p_mb = proj(OFF_MB, D_MODEL)
    for ci in range(gla_chunks):
        rs = slice(ci * cc, (ci + 1) * cc)
        o_b = o_attn[ci] * _silu(p_bz[rs])
        y_b = _dot(o_b.astype(BF16), w_sq_ref[1, :, 0:D_MODEL])
        g_b = _sigmoid(p_mb[rs] + bm_ref[:, D_MODEL:2 * D_MODEL])
        merged_c = merged[rs] + g_b * y_b
        out = _dot(merged_c.astype(BF16), w_sq_ref[2, :, 0:D_MODEL])
        o_ref[0, rs, :] = _rms(x_ref[0, rs, :] + gate * out) * fg_ref[...]


def _const_spec(shape):
    nd = len(shape)
    return pl.BlockSpec(shape, lambda b, j: (0,) * nd, pipeline_mode=pl.Buffered(1))


def kernel(x, c, w_ada, b_ada, norm_g, w_in, a_ln_g, a_ln_b, a_w_s, a_b_s, b_w_gate2, b_b_gate, b_norm_g, w_proj_a, w_proj_b, b_merge, w_out, final_g):
    bsz, seq, d = x.shape
    assert d == D_MODEL and seq % TILE_M == 0 and w_ada.shape[0] == 1 and bsz <= SUBLANES
    l = 0

    row2 = lambda v: v.reshape(1, -1)
    in_hbm = {2, 5, 13, 14, 16}
    operands = (
        x, c, w_ada, b_ada, row2(norm_g[l]), jnp.swapaxes(w_in, 1, 2), row2(a_ln_g[l]), row2(a_ln_b[l]),
        a_w_s[l], a_b_s[l], b_w_gate2[l], row2(b_b_gate[l]), row2(b_norm_g[l]), w_proj_a, w_proj_b,
        row2(b_merge[l]), w_out, row2(final_g),
    )
    in_specs = [pl.BlockSpec((1, TILE_M, d), lambda b, j: (b, j, 0))] + [
        pl.BlockSpec(memory_space=pl.ANY) if i in in_hbm else _const_spec(op.shape)
        for i, op in enumerate(operands) if i >= 1]

    return pl.pallas_call(
        _block_kernel,
        grid=(bsz, seq // TILE_M),
        in_specs=in_specs,
        out_specs=pl.BlockSpec((1, TILE_M, d), lambda b, j: (b, j, 0)),
        out_shape=jax.ShapeDtypeStruct((bsz, seq, d), x.dtype),
        scratch_shapes=[pltpu.VMEM((B_HEADS, B_HEAD_K, B_HEAD_V), F32),
                        pltpu.VMEM((TILE_M // GLA_CHUNK * B_HEADS, GLA_CHUNK, GLA_CHUNK), BF16),
                        pltpu.VMEM((d, W_IN_RESIDENT), BF16),
                        pltpu.VMEM((3, d, W_SQ_RESIDENT), BF16),
                        pltpu.VMEM((SUBLANES, 3 * d), F32),
                        pltpu.VMEM((A_GROUPS, A_CHUNK, A_GROUP_DIM), F32),
                        pltpu.VMEM((TILE_M, d), BF16),
                        pltpu.VMEM((STAGE_SLOTS, STAGE_ROWS, D_MODEL), F32),
                        pltpu.SemaphoreType.DMA((STAGE_SLOTS,))],
        compiler_params=pltpu.CompilerParams(
            dimension_semantics=("arbitrary", "arbitrary"),
            vmem_limit_bytes=VMEM_LIMIT_BYTES),
        name="hybrid_block",
    )(*operands)
```
